```python
import math
import jax, jax.numpy as jnp
from jax import lax
import numpy as np

D_MODEL = 1024
BATCH = 4
SEQ = 4096
DEPTH = 4
DEC_BATCH = 128
DEC_SEQ = 1
PAST_LEN = 8192
PAGE_SIZE = 128

N_MIXERS = 2
N_MLA_LAYERS = (DEPTH + N_MIXERS - 1) // N_MIXERS
N_SWA_LAYERS = DEPTH // N_MIXERS
NORM_EPS = 1e-6
NEG_INF = -1e30

MLA_HEADS = 16
Q_LORA = 512
KV_LORA = 256
QK_NOPE = 128
QK_ROPE = 64
V_HEAD = 128
ROPE_THETA = 10000.0
MLA_QBLOCK = 128
MLA_WIDTH = MLA_HEADS * V_HEAD
MLA_IN = Q_LORA + KV_LORA + QK_ROPE + MLA_WIDTH
MLA_SCALE = 1.0 / math.sqrt(QK_NOPE + QK_ROPE)

SWA_HEADS = 16
SWA_KV_HEADS = 4
SWA_GROUP = SWA_HEADS // SWA_KV_HEADS
SWA_HEAD_DIM = 64
WINDOW = 128
SWA_Q = SWA_HEADS * SWA_HEAD_DIM
SWA_KV = SWA_KV_HEADS * SWA_HEAD_DIM
SWA_IN = 2 * SWA_Q + 2 * SWA_KV
SWA_SCALE = 1.0 / math.sqrt(SWA_HEAD_DIM)

kernel_name = 'hybrid_mla_swa_sink_decoder_step'


def rmsnorm(x, g):
    xf = x.astype(jnp.float32)
    y = xf * lax.rsqrt(jnp.mean(xf * xf, axis=-1, keepdims=True) + NORM_EPS)
    return (y * g.astype(jnp.float32)).astype(x.dtype)


def rope_tables(pos):
    inv = ROPE_THETA ** (-jnp.arange(0, QK_ROPE, 2, dtype=jnp.float32) / QK_ROPE)
    ang = pos.astype(jnp.float32)[:, None] * inv[None, :]
    ang = jnp.concatenate([ang, ang], axis=-1)
    return jnp.cos(ang), jnp.sin(ang)


def apply_rope(x, cos, sin):
    xf = x.astype(jnp.float32)
    half = QK_ROPE // 2
    rot = jnp.concatenate([-xf[..., half:], xf[..., :half]], axis=-1)
    return (xf * cos + rot * sin).astype(x.dtype)


def alibi_slopes():
    h = jnp.arange(1, SWA_HEADS + 1, dtype=jnp.float32)
    return (2.0 ** (-8.0 * h / SWA_HEADS)).reshape(SWA_KV_HEADS, SWA_GROUP)


def gated_out(o, gate, w_o):
    return (o * jax.nn.silu(gate)) @ w_o


def mla_project(h, pos, w_in, q_norm, w_qb, kv_norm):
    z = h @ w_in
    cq = rmsnorm(z[..., :Q_LORA], q_norm)
    ckv = rmsnorm(z[..., Q_LORA:Q_LORA + KV_LORA], kv_norm)
    kpe = z[..., Q_LORA + KV_LORA:Q_LORA + KV_LORA + QK_ROPE]
    gate = z[..., Q_LORA + KV_LORA + QK_ROPE:]
    q = jnp.einsum('bsc,chd->bshd', cq, w_qb)
    q_nope, q_rope = q[..., :QK_NOPE], q[..., QK_NOPE:]
    cos, sin = rope_tables(pos)
    q_rope = apply_rope(q_rope, cos[:, None, :], sin[:, None, :])
    kpe = apply_rope(kpe, cos, sin)
    return q_nope, q_rope, ckv, kpe, gate


def mla_prompt_attention(q_nope, q_rope, ckv, kpe, w_uk, w_uv):
    b, s = q_nope.shape[:2]
    k_nope = jnp.einsum('bsc,chn->bshn', ckv, w_uk)
    v = jnp.einsum('bsc,chv->bshv', ckv, w_uv)
    nb = s // MLA_QBLOCK
    qn = q_nope.reshape(b, nb, MLA_QBLOCK, MLA_HEADS, QK_NOPE).swapaxes(0, 1)
    qr = q_rope.reshape(b, nb, MLA_QBLOCK, MLA_HEADS, QK_ROPE).swapaxes(0, 1)
    kpos = jnp.arange(s, dtype=jnp.int32)

    def block(args):
        qn_b, qr_b, blk = args
        sc = (jnp.einsum('bqhn,bkhn->bhqk', qn_b, k_nope)
              + jnp.einsum('bqhr,bkr->bhqk', qr_b, kpe)).astype(jnp.float32) * MLA_SCALE
        qpos = blk * MLA_QBLOCK + jnp.arange(MLA_QBLOCK, dtype=jnp.int32)
        sc = jnp.where(kpos[None, :] <= qpos[:, None], sc, NEG_INF)
        p = jax.nn.softmax(sc, axis=-1).astype(v.dtype)
        return jnp.einsum('bhqk,bkhv->bqhv', p, v)

    o = lax.map(block, (qn, qr, jnp.arange(nb, dtype=jnp.int32)))
    return o.swapaxes(0, 1).reshape(b, s, MLA_HEADS, V_HEAD)


def mla_sample_attention(q_nope, q_rope, ckv_new, kpe_new, past_ckv, past_kpe, w_uk, w_uv):
    t = q_nope.shape[1]
    past_len = past_ckv.shape[1]
    q_lat = jnp.einsum('bthn,chn->bthc', q_nope, w_uk)
    s_past = (jnp.einsum('bthc,bsc->bhts', q_lat, past_ckv)
              + jnp.einsum('bthr,bsr->bhts', q_rope, past_kpe)).astype(jnp.float32) * MLA_SCALE
    s_new = (jnp.einsum('bthc,buc->bhtu', q_lat, ckv_new)
             + jnp.einsum('bthr,bur->bhtu', q_rope, kpe_new)).astype(jnp.float32) * MLA_SCALE
    causal = jnp.arange(t)[None, :] <= jnp.arange(t)[:, None]
    s_new = jnp.where(causal, s_new, NEG_INF)
    p = jax.nn.softmax(jnp.concatenate([s_past, s_new], axis=-1), axis=-1).astype(ckv_new.dtype)
    o_lat = (jnp.einsum('bhts,bsc->bthc', p[..., :past_len], past_ckv)
             + jnp.einsum('bhtu,buc->bthc', p[..., past_len:], ckv_new))
    return jnp.einsum('bthc,chv->bthv', o_lat, w_uv)


def swa_project(h, w_in):
    b, s = h.shape[:2]
    z = h @ w_in
    q = z[..., :SWA_Q].reshape(b, s, SWA_KV_HEADS, SWA_GROUP, SWA_HEAD_DIM)
    k = z[..., SWA_Q:SWA_Q + SWA_KV].reshape(b, s, SWA_KV_HEADS, SWA_HEAD_DIM)
    v = z[..., SWA_Q + SWA_KV:SWA_Q + 2 * SWA_KV].reshape(b, s, SWA_KV_HEADS, SWA_HEAD_DIM)
    gate = z[..., SWA_Q + 2 * SWA_KV:]
    return q, k, v, gate


def sink_softmax(sc, valid, sinks):
    sc = jnp.where(valid, sc, NEG_INF)
    sk = sinks.astype(jnp.float32).reshape(SWA_KV_HEADS, SWA_GROUP)[:, :, None, None]
    m = jnp.maximum(jnp.max(sc, axis=-1, keepdims=True), sk)
    e = jnp.exp(sc - m)
    return e / (jnp.sum(e, axis=-1, keepdims=True) + jnp.exp(sk - m))


def swa_prompt_attention(q, k, v, sinks):
    b, s = q.shape[:2]
    nb = s // WINDOW
    qb = q.reshape(b, nb, WINDOW, SWA_KV_HEADS, SWA_GROUP, SWA_HEAD_DIM)
    kb = k.reshape(b, nb, WINDOW, SWA_KV_HEADS, SWA_HEAD_DIM)
    vb = v.reshape(b, nb, WINDOW, SWA_KV_HEADS, SWA_HEAD_DIM)
    kk = jnp.concatenate([jnp.concatenate([jnp.zeros_like(kb[:, :1]), kb[:, :-1]], axis=1), kb], axis=2)
    vv = jnp.concatenate([jnp.concatenate([jnp.zeros_like(vb[:, :1]), vb[:, :-1]], axis=1), vb], axis=2)
    sc = jnp.einsum('bnqkgd,bnskd->bnkgqs', qb, kk).astype(jnp.float32) * SWA_SCALE
    qi = WINDOW + jnp.arange(WINDOW)
    ki = jnp.arange(2 * WINDOW)
    dist = qi[:, None] - ki[None, :]
    in_win = (dist >= 0) & (dist < WINDOW)
    blk_ok = (jnp.arange(nb)[:, None] > 0) | (ki[None, :] >= WINDOW)
    valid = in_win[None] & blk_ok[:, None, :]
    sc = sc - alibi_slopes()[:, :, None, None] * dist.astype(jnp.float32)
    p = sink_softmax(sc, valid[None, :, None, None], sinks).astype(v.dtype)
    o = jnp.einsum('bnkgqs,bnskd->bnqkgd', p, vv)
    return o.reshape(b, s, SWA_Q)


def swa_sample_attention(q, k, v, buf_k, buf_v, sinks, past_len):
    b, t = q.shape[:2]
    nbuf = buf_k.shape[1]
    kk = jnp.concatenate([buf_k, k], axis=1)
    vv = jnp.concatenate([buf_v, v], axis=1)
    kpos = past_len - nbuf + jnp.arange(nbuf + t)
    qpos = past_len + jnp.arange(t)
    dist = qpos[:, None] - kpos[None, :]
    valid = (dist >= 0) & (dist < WINDOW)
    sc = jnp.einsum('btkgd,bskd->bkgts', q, kk).astype(jnp.float32) * SWA_SCALE
    sc = sc - alibi_slopes()[:, :, None, None] * dist.astype(jnp.float32)
    p = sink_softmax(sc, valid, sinks).astype(v.dtype)
    o = jnp.einsum('bkgts,bskd->btkgd', p, vv)
    return o.reshape(b, t, SWA_Q), kk[:, t:], vv[:, t:]


def setup_inputs(seed: int = 0) -> dict:
    key = jax.random.key(seed)
    ks = jax.random.split(key, 24)
    n_pages = PAST_LEN // PAGE_SIZE
    n_used = DEC_BATCH * n_pages
    n_phys = (5 * n_used + 3) // 4
    swa_buf = min(WINDOW, PAST_LEN)

    def nrm(k, shape, scale=1.0):
        return jax.random.normal(k, shape, jnp.float32) * scale

    def gain(k, shape):
        return 1.0 + 0.05 * jax.random.normal(k, shape, jnp.float32)

    perm = jax.random.permutation(ks[0], n_phys)
    page_table = perm[:n_used].reshape(DEC_BATCH, n_pages).astype(jnp.int32)
    return {
        'x_prompt': nrm(ks[1], (BATCH, SEQ, D_MODEL)),
        'x_sample': nrm(ks[2], (DEC_BATCH, DEC_SEQ, D_MODEL)),
        'cache_ckv': nrm(ks[3], (N_MLA_LAYERS, n_phys, PAGE_SIZE, KV_LORA)),
        'cache_kpe': nrm(ks[4], (N_MLA_LAYERS, n_phys, PAGE_SIZE, QK_ROPE)),
        'state_swa_k': nrm(ks[5], (N_SWA_LAYERS, DEC_BATCH, swa_buf, SWA_KV_HEADS, SWA_HEAD_DIM)),
        'state_swa_v': nrm(ks[6], (N_SWA_LAYERS, DEC_BATCH, swa_buf, SWA_KV_HEADS, SWA_HEAD_DIM)),
        'page_table': page_table,
        'pre_norm': gain(ks[7], (DEPTH, D_MODEL)),
        'post_norm': gain(ks[8], (DEPTH, D_MODEL)),
        'mla_w_in': nrm(ks[9], (N_MLA_LAYERS, D_MODEL, MLA_IN), D_MODEL ** -0.5),
        'mla_q_norm': gain(ks[10], (N_MLA_LAYERS, Q_LORA)),
        'mla_w_qb': nrm(ks[11], (N_MLA_LAYERS, Q_LORA, MLA_HEADS, QK_NOPE + QK_ROPE), Q_LORA ** -0.5),
        'mla_kv_norm': gain(ks[12], (N_MLA_LAYERS, KV_LORA)),
        'mla_w_uk': nrm(ks[13], (N_MLA_LAYERS, KV_LORA, MLA_HEADS, QK_NOPE), KV_LORA ** -0.5),
        'mla_w_uv': nrm(ks[14], (N_MLA_LAYERS, KV_LORA, MLA_HEADS, V_HEAD), KV_LORA ** -0.5),
        'mla_w_o': nrm(ks[15], (N_MLA_LAYERS, MLA_WIDTH, D_MODEL), MLA_WIDTH ** -0.5),
        'swa_w_in': nrm(ks[16], (N_SWA_LAYERS, D_MODEL, SWA_IN), D_MODEL ** -0.5),
        'swa_sinks': nrm(ks[17], (N_SWA_LAYERS, SWA_HEADS)),
        'swa_w_o': nrm(ks[18], (N_SWA_LAYERS, SWA_Q, D_MODEL), SWA_Q ** -0.5),
    }


def reference(x_prompt, x_sample, cache_ckv, cache_kpe, state_swa_k, state_swa_v, page_table,
              pre_norm, post_norm, mla_w_in, mla_q_norm, mla_w_qb, mla_kv_norm, mla_w_uk,
              mla_w_uv, mla_w_o, swa_w_in, swa_sinks, swa_w_o):
    n_b, seq, _ = x_prompt.shape
    n_db, t_new, _ = x_sample.shape
    past_len = page_table.shape[1] * PAGE_SIZE
    pos_p = jnp.arange(seq, dtype=jnp.int32)
    pos_s = past_len + jnp.arange(t_new, dtype=jnp.int32)
    hp, hs = x_prompt, x_sample
    ckv_p, kpe_p, ckv_s, kpe_s = [], [], [], []
    swk_p, swv_p, swk_s, swv_s = [], [], [], []
    for i in range(DEPTH):
        ap = rmsnorm(hp, pre_norm[i])
        a_s = rmsnorm(hs, pre_norm[i])
        l = i // N_MIXERS
        if i % N_MIXERS == 0:
            qn, qr, ckv, kpe, gate = mla_project(ap, pos_p, mla_w_in[l], mla_q_norm[l],
                                                 mla_w_qb[l], mla_kv_norm[l])
            o = mla_prompt_attention(qn, qr, ckv, kpe, mla_w_uk[l], mla_w_uv[l])
            mp = gated_out(o.reshape(n_b, seq, MLA_WIDTH), gate, mla_w_o[l])
            ckv_p.append(ckv)
            kpe_p.append(kpe)
            qn, qr, ckv, kpe, gate = mla_project(a_s, pos_s, mla_w_in[l], mla_q_norm[l],
                                                 mla_w_qb[l], mla_kv_norm[l])
            past_ckv = cache_ckv[l, page_table].reshape(n_db, past_len, KV_LORA)
            past_kpe = cache_kpe[l, page_table].reshape(n_db, past_len, QK_ROPE)
            o = mla_sample_attention(qn, qr, ckv, kpe, past_ckv, past_kpe, mla_w_uk[l], mla_w_uv[l])
            ms = gated_out(o.reshape(n_db, t_new, MLA_WIDTH), gate, mla_w_o[l])
            ckv_s.append(ckv)
            kpe_s.append(kpe)
        else:
            q, k, v, gate = swa_project(ap, swa_w_in[l])
            o = swa_prompt_attention(q, k, v, swa_sinks[l])
            mp = gated_out(o, gate, swa_w_o[l])
            keep = min(WINDOW, seq)
            swk_p.append(k[:, seq - keep:])
            swv_p.append(v[:, seq - keep:])
            q, k, v, gate = swa_project(a_s, swa_w_in[l])
            o, nk, nv = swa_sample_attention(q, k, v, state_swa_k[l], state_swa_v[l],
                                             swa_sinks[l], past_len)
            ms = gated_out(o, gate, swa_w_o[l])
            swk_s.append(nk)
            swv_s.append(nv)
        hp = hp + rmsnorm(mp, post_norm[i])
        hs = hs + rmsnorm(ms, post_norm[i])
    return (hp, hs, jnp.stack(ckv_p), jnp.stack(kpe_p), jnp.stack(ckv_s), jnp.stack(kpe_s),
            jnp.stack(swk_p), jnp.stack(swv_p), jnp.stack(swk_s), jnp.stack(swv_s))
```

```python
import functools
import math

import jax
import jax.numpy as jnp
import numpy as np
from jax import lax
from jax.experimental import pallas as pl
from jax.experimental.pallas import tpu as pltpu

F32 = jnp.float32
BF16 = jnp.bfloat16

D_MODEL = 1024
NORM_EPS = 1e-6
NEG_INF = -1e30
PAGE_SIZE = 128

MLA_HEADS = 16
Q_LORA = 512
KV_LORA = 256
QK_NOPE = 128
QK_ROPE = 64
V_HEAD = 128
ROPE_THETA = 10000.0
MLA_WIDTH = MLA_HEADS * V_HEAD
MLA_SCALE = 1.0 / math.sqrt(QK_NOPE + QK_ROPE)

SWA_HEADS = 16
SWA_KV_HEADS = 4
SWA_GROUP = SWA_HEADS // SWA_KV_HEADS
SWA_HEAD_DIM = 64
WINDOW = 128
SWA_Q = SWA_HEADS * SWA_HEAD_DIM
SWA_KV = SWA_KV_HEADS * SWA_HEAD_DIM
SWA_SCALE = 1.0 / math.sqrt(SWA_HEAD_DIM)

LANES = 128
VMEM_LIMIT = 56 * 1024 * 1024

SWA_HEAD_ORDER = tuple(
    (2 * i + par) * SWA_GROUP + g
    for i in range(SWA_KV_HEADS // 2) for g in range(SWA_GROUP) for par in range(2))

NT_DIMS = (((1,), (1,)), ((), ()))


def _params(*sem):
    return pltpu.CompilerParams(dimension_semantics=sem, vmem_limit_bytes=VMEM_LIMIT)


def _rms(x, g):
    return x * lax.rsqrt(jnp.mean(x * x, axis=-1, keepdims=True) + NORM_EPS) * g


def _silu(x):
    return x * (1.0 / (1.0 + jnp.exp(-x)))


def _full(shape):
    return pl.BlockSpec(shape, lambda *_: (0,) * len(shape))


def _mla_in_kernel(h_ref, g_ref, w_ref, qn_ref, kvn_ref, cs_ref,
                   cq_ref, ckv_ref, ckvb_ref, kpe_ref, kpeb_ref, sg_ref):
    a = _rms(h_ref[...], g_ref[...]).astype(BF16)
    z = jnp.dot(a, w_ref[...], preferred_element_type=F32)
    cq_ref[...] = _rms(z[:, :Q_LORA], qn_ref[...]).astype(BF16)
    ckv = _rms(z[:, Q_LORA:Q_LORA + KV_LORA], kvn_ref[...])
    ckv_ref[...] = ckv
    ckvb_ref[...] = ckv.astype(BF16)
    g0 = Q_LORA + KV_LORA
    sg_ref[...] = _silu(z[:, g0:g0 + MLA_WIDTH])
    t = z[:, g0 + MLA_WIDTH:] * cs_ref[...]
    kpe2 = t + pltpu.roll(t, QK_ROPE, axis=1)
    kpe_ref[...] = kpe2
    kpeb_ref[...] = kpe2.astype(BF16)


def _mla_in(h, g, w, qn, kvn, cs, tm):
    rows = h.shape[0]
    n_cs = cs.shape[0] // tm
    n_in = w.shape[1]
    row = lambda r: (r, 0)
    outs = (
        jax.ShapeDtypeStruct((rows, Q_LORA), BF16),
        jax.ShapeDtypeStruct((rows, KV_LORA), F32),
        jax.ShapeDtypeStruct((rows, KV_LORA), BF16),
        jax.ShapeDtypeStruct((rows, LANES), F32),
        jax.ShapeDtypeStruct((rows, LANES), BF16),
        jax.ShapeDtypeStruct((rows, MLA_WIDTH), F32),
    )
    return pl.pallas_call(
        _mla_in_kernel,
        grid=(rows // tm,),
        in_specs=[
            pl.BlockSpec((tm, D_MODEL), row),
            _full((1, D_MODEL)),
            _full((D_MODEL, n_in)),
            _full((1, Q_LORA)),
            _full((1, KV_LORA)),
            pl.BlockSpec((tm, LANES), lambda r: (r % n_cs, 0)),
        ],
        out_specs=tuple(pl.BlockSpec((tm, o.shape[1]), row) for o in outs),
        out_shape=outs,
        compiler_params=_params("parallel"),
        name="mla_in",
    )(h, g, w, qn, kvn, cs)


def _rope_groups(cq, wr_ref, wrot_ref, cc, ss):
    r = jnp.dot(cq, wr_ref[...], preferred_element_type=F32)
    rr = jnp.dot(cq, wrot_ref[...], preferred_element_type=F32)
    n_grp = r.shape[1] // LANES
    return [r[:, j * LANES:(j + 1) * LANES] * cc + rr[:, j * LANES:(j + 1) * LANES] * ss
            for j in range(n_grp)]


def _mla_qkv_kernel(cq_ref, ckvb_ref, wqn_ref, wqr_ref, wqrot_ref, wuk_ref, wuv_ref,
                    cc_ref, ss_ref, qn_ref, qr2_ref, kn_ref, v_ref):
    cq = cq_ref[...]
    qn_ref[...] = jnp.dot(cq, wqn_ref[...], preferred_element_type=F32).astype(BF16)
    groups = _rope_groups(cq, wqr_ref, wqrot_ref, cc_ref[...], ss_ref[...])
    low = lax.broadcasted_iota(jnp.int32, groups[0].shape, 1) < QK_ROPE
    for j, grp in enumerate(groups):
        qr2_ref[:, (2 * j) * LANES:(2 * j + 1) * LANES] = jnp.where(low, grp, 0.0).astype(BF16)
        qr2_ref[:, (2 * j + 1) * LANES:(2 * j + 2) * LANES] = jnp.where(low, 0.0, grp).astype(BF16)
    ckv = ckvb_ref[...]
    kn_ref[...] = jnp.dot(ckv, wuk_ref[...], preferred_element_type=F32).astype(BF16)
    v_ref[...] = jnp.dot(ckv, wuv_ref[...], preferred_element_type=F32).astype(BF16)


def _mla_qkv(cq, ckvb, wqn, wqr, wqrot, wuk, wuv, cc, ss, tm):
    rows = cq.shape[0]
    n_cs = cc.shape[0] // tm
    row = lambda r: (r, 0)
    tab = pl.BlockSpec((tm, LANES), lambda r: (r % n_cs, 0))
    wide = jax.ShapeDtypeStruct((rows, MLA_WIDTH), BF16)
    return pl.pallas_call(
        _mla_qkv_kernel,
        grid=(rows // tm,),
        in_specs=[
            pl.BlockSpec((tm, Q_LORA), row),
            pl.BlockSpec((tm, KV_LORA), row),
            _full(wqn.shape), _full(wqr.shape), _full(wqrot.shape),
            _full(wuk.shape), _full(wuv.shape), tab, tab,
        ],
        out_specs=(pl.BlockSpec((tm, MLA_WIDTH), row),) * 4,
        out_shape=(wide,) * 4,
        compiler_params=_params("parallel"),
        name="mla_qkv",
    )(cq, ckvb, wqn, wqr, wqrot, wuk, wuv, cc, ss)


def _mla_q_kernel(cq_ref, wqn_ref, wqr_ref, wqrot_ref, cc_ref, ss_ref, qn_ref, qr_ref):
    cq = cq_ref[...]
    qn_ref[...] = jnp.dot(cq, wqn_ref[...], preferred_element_type=F32).astype(BF16)
    groups = _rope_groups(cq, wqr_ref, wqrot_ref, cc_ref[...], ss_ref[...])
    for j, grp in enumerate(groups):
        qr_ref[:, j * LANES:(j + 1) * LANES] = grp.astype(BF16)


def _mla_q(cq, wqn, wqr, wqrot, cc, ss):
    rows = cq.shape[0]
    return pl.pallas_call(
        _mla_q_kernel,
        grid=(1,),
        in_specs=[_full(cq.shape), _full(wqn.shape), _full(wqr.shape), _full(wqrot.shape),
                  _full(cc.shape), _full(ss.shape)],
        out_specs=(_full((rows, MLA_WIDTH)), _full((rows, MLA_HEADS * QK_ROPE))),
        out_shape=(jax.ShapeDtypeStruct((rows, MLA_WIDTH), BF16),
                   jax.ShapeDtypeStruct((rows, MLA_HEADS * QK_ROPE), BF16)),
        compiler_params=_params("arbitrary"),
        name="mla_q_sample",
    )(cq, wqn, wqr, wqrot, cc, ss)


def _mla_flash_kernel(qn_ref, qr_ref, kn_ref, kpe_ref, v_ref, sg_ref, o_ref, *, blk):
    i = pl.program_id(2)
    q = jnp.concatenate([qn_ref[...], qr_ref[...]], axis=-1)

    def chunk(j, carry, diagonal):
        m, l, acc = carry
        rows = pl.ds(pl.multiple_of(j * blk, blk), blk)
        k = jnp.concatenate([kn_ref[rows, :], kpe_ref[rows, :]], axis=-1)
        s = lax.dot_general(q, k, NT_DIMS, preferred_element_type=F32) * MLA_SCALE
        if diagonal:
            qi = lax.broadcasted_iota(jnp.int32, s.shape, 0)
            ki = lax.broadcasted_iota(jnp.int32, s.shape, 1)
            s = jnp.where(ki <= qi, s, NEG_INF)
        m_new = jnp.maximum(m, jnp.max(s, axis=-1, keepdims=True))
        alpha = jnp.exp(m - m_new)
        p = jnp.exp(s - m_new)
        l = alpha * l + jnp.sum(p, axis=-1, keepdims=True)
        acc = alpha * acc + jnp.dot(p.astype(BF16), v_ref[rows, :], preferred_element_type=F32)
        return m_new, l, acc

    init = (jnp.full((blk, 1), NEG_INF, F32), jnp.zeros((blk, 1), F32),
            jnp.zeros((blk, V_HEAD), F32))
    carry = lax.fori_loop(0, i, lambda j, c: chunk(j, c, False), init)
    _, l, acc = chunk(i, carry, True)
    o_ref[...] = (acc / l * sg_ref[...]).astype(BF16)


def _mla_flash(qn, qr2, kn, kpeb, v, sg, blk):
    b, s, _ = qn.shape
    q_spec = pl.BlockSpec((None, blk, LANES), lambda bi, h, i: (bi, i, h))
    kv_spec = pl.BlockSpec((None, s, LANES), lambda bi, h, i: (bi, 0, h))
    return pl.pallas_call(
        functools.partial(_mla_flash_kernel, blk=blk),
        grid=(b, MLA_HEADS, s // blk),
        in_specs=[q_spec, q_spec, kv_spec,
                  pl.BlockSpec((None, s, LANES), lambda bi, h, i: (bi, 0, 0)),
                  kv_spec, q_spec],
        out_specs=q_spec,
        out_shape=jax.ShapeDtypeStruct((b, s, MLA_WIDTH), BF16),
        compiler_params=_params("parallel", "parallel", "arbitrary"),
        name="mla_flash",
    )(qn, qr2, kn, kpeb, v, sg)


def _out_kernel(x_ref, w_ref, g_ref, h_ref, o_ref):
    y = jnp.dot(x_ref[...], w_ref[...], preferred_element_type=F32)
    o_ref[...] = h_ref[...] + _rms(y, g_ref[...])


def _out_proj(x, w, g, h, tm):
    rows, width = x.shape
    row = lambda r: (r, 0)
    return pl.pallas_call(
        _out_kernel,
        grid=(rows // tm,),
        in_specs=[pl.BlockSpec((tm, width), row), _full(w.shape), _full((1, D_MODEL)),
                  pl.BlockSpec((tm, D_MODEL), row)],
        out_specs=pl.BlockSpec((tm, D_MODEL), row),
        out_shape=jax.ShapeDtypeStruct((rows, D_MODEL), F32),
        compiler_params=_params("parallel"),
        name="out_proj",
    )(x, w, g, h)


def _qlat_kernel(qn_ref, wukt_ref, o_ref):
    o_ref[...] = jnp.dot(qn_ref[...], wukt_ref[...], preferred_element_type=F32).astype(BF16)


def _qlat(qn, wukt):
    rows = qn.shape[0]
    return pl.pallas_call(
        _qlat_kernel,
        grid=(MLA_HEADS,),
        in_specs=[pl.BlockSpec((rows, QK_NOPE), lambda h: (0, h)),
                  pl.BlockSpec((None, QK_NOPE, KV_LORA), lambda h: (h, 0, 0))],
        out_specs=pl.BlockSpec((rows, KV_LORA), lambda h: (0, h)),
        out_shape=jax.ShapeDtypeStruct((rows, MLA_HEADS * KV_LORA), BF16),
        compiler_params=_params("parallel"),
        name="mla_qlat",
    )(qn, wukt)


def _paged_kernel(pt_ref, ql_ref, qr_ref, cnew_ref, knew_ref, *rest, n_pages):
    ckv_refs = rest[:n_pages]
    kpe_refs = rest[n_pages:2 * n_pages]
    o_ref, kbuf, pbuf, m_sc, l_sc, acc_sc = rest[2 * n_pages:]
    c = pl.program_id(1)
    ql = ql_ref[...]
    qr = qr_ref[...]

    @pl.when(c == 0)
    def _():
        cn = cnew_ref[...].astype(BF16).astype(F32)
        kn = knew_ref[...].astype(BF16).astype(F32)
        s_new = (jnp.sum(ql.astype(F32) * cn, axis=-1, keepdims=True)
                 + jnp.sum(qr.astype(F32) * kn, axis=-1, keepdims=True)) * MLA_SCALE
        m_sc[...] = s_new
        l_sc[...] = jnp.ones_like(s_new)
        acc_sc[...] = jnp.broadcast_to(cn, acc_sc.shape)

    for p in range(n_pages):
        kbuf[p * PAGE_SIZE:(p + 1) * PAGE_SIZE, :] = ckv_refs[p][...].astype(BF16)
        pbuf[p * PAGE_SIZE:(p + 1) * PAGE_SIZE, :] = kpe_refs[p][...].astype(BF16)
    kk = kbuf[...]
    s = (lax.dot_general(ql, kk, NT_DIMS, preferred_element_type=F32)
         + lax.dot_general(qr, pbuf[...], NT_DIMS, preferred_element_type=F32)) * MLA_SCALE
    m_old = m_sc[...]
    m_new = jnp.maximum(m_old, jnp.max(s, axis=-1, keepdims=True))
    alpha = jnp.exp(m_old - m_new)
    p_ = jnp.exp(s - m_new)
    l_new = alpha * l_sc[...] + jnp.sum(p_, axis=-1, keepdims=True)
    acc = alpha * acc_sc[...] + jnp.dot(p_.astype(BF16), kk, preferred_element_type=F32)
    m_sc[...] = m_new
    l_sc[...] = l_new
    acc_sc[...] = acc

    @pl.when(c == pl.num_programs(1) - 1)
    def _():
        o_ref[...] = acc / l_new


def _paged_attention(page_table, qlat3, qr3, ckv_new, kpe_new, cache_ckv, cache_kpe, layer,
                     n_pages):
    nb, pages_per_seq = page_table.shape
    n_chunks = pages_per_seq // n_pages

    def page_spec(width, p):
        return pl.BlockSpec((None, None, PAGE_SIZE, width),
                            lambda b, c, pt: (layer, pt[b, c * n_pages + p], 0, 0))

    per_b = lambda shape: pl.BlockSpec((None,) + shape, lambda b, c, pt: (b, 0, 0))
    grid_spec = pltpu.PrefetchScalarGridSpec(
        num_scalar_prefetch=1,
        grid=(nb, n_chunks),
        in_specs=[per_b((MLA_HEADS, KV_LORA)), per_b((MLA_HEADS, QK_ROPE)),
                  per_b((1, KV_LORA)), per_b((1, QK_ROPE))]
        + [page_spec(KV_LORA, p) for p in range(n_pages)]
        + [page_spec(QK_ROPE, p) for p in range(n_pages)],
        out_specs=per_b((MLA_HEADS, KV_LORA)),
        scratch_shapes=[
            pltpu.VMEM((n_pages * PAGE_SIZE, KV_LORA), BF16),
            pltpu.VMEM((n_pages * PAGE_SIZE, QK_ROPE), BF16),
            pltpu.VMEM((MLA_HEADS, 1), F32),
            pltpu.VMEM((MLA_HEADS, 1), F32),
            pltpu.VMEM((MLA_HEADS, KV_LORA), F32),
        ],
    )
    return pl.pallas_call(
        functools.partial(_paged_kernel, n_pages=n_pages),
        grid_spec=grid_spec,
        out_shape=jax.ShapeDtypeStruct((nb, MLA_HEADS, KV_LORA), F32),
        compiler_params=_params("parallel", "arbitrary"),
        name="mla_paged",
    )(page_table, qlat3, qr3, ckv_new, kpe_new,
      *([cache_ckv] * n_pages), *([cache_kpe] * n_pages))


def _olat_kernel(ol_ref, wuv_ref, sg_ref, o_ref):
    o = jnp.dot(ol_ref[...].astype(BF16), wuv_ref[...], preferred_element_type=F32)
    o_ref[...] = (o * sg_ref[...]).astype(BF16)


def _olat_proj(olat, wuv3, sg):
    rows = olat.shape[0]
    return pl.pallas_call(
        _olat_kernel,
        grid=(MLA_HEADS,),
        in_specs=[pl.BlockSpec((rows, KV_LORA), lambda h: (0, h)),
                  pl.BlockSpec((None, KV_LORA, V_HEAD), lambda h: (h, 0, 0)),
                  pl.BlockSpec((rows, V_HEAD), lambda h: (0, h))],
        out_specs=pl.BlockSpec((rows, V_HEAD), lambda h: (0, h)),
        out_shape=jax.ShapeDtypeStruct((rows, MLA_WIDTH), BF16),
        compiler_params=_params("parallel"),
        name="mla_olat",
    )(olat, wuv3, sg)


def _swa_in_kernel(h_ref, g_ref, w_ref, q_ref, k_ref, v_ref, sg_ref):
    a = _rms(h_ref[...], g_ref[...]).astype(BF16)
    z = jnp.dot(a, w_ref[...], preferred_element_type=F32)
    q_ref[...] = z[:, :SWA_Q].astype(BF16)
    k_ref[...] = z[:, SWA_Q:SWA_Q + SWA_KV]
    v_ref[...] = z[:, SWA_Q + SWA_KV:SWA_Q + 2 * SWA_KV]
    sg_ref[...] = _silu(z[:, SWA_Q + 2 * SWA_KV:])


def _swa_in(h, g, w, tm):
    rows = h.shape[0]
    row = lambda r: (r, 0)
    outs = (jax.ShapeDtypeStruct((rows, SWA_Q), BF16),
            jax.ShapeDtypeStruct((rows, SWA_KV), F32),
            jax.ShapeDtypeStruct((rows, SWA_KV), F32),
            jax.ShapeDtypeStruct((rows, SWA_Q), F32))
    return pl.pallas_call(
        _swa_in_kernel,
        grid=(rows // tm,),
        in_specs=[pl.BlockSpec((tm, D_MODEL), row), _full((1, D_MODEL)), _full(w.shape)],
        out_specs=tuple(pl.BlockSpec((tm, o.shape[1]), row) for o in outs),
        out_shape=outs,
        compiler_params=_params("parallel"),
        name="swa_in",
    )(h, g, w)


def _sink_softmax(s, sink):
    m = jnp.maximum(jnp.max(s, axis=-1, keepdims=True), sink)
    e = jnp.exp(s - m)
    return e / (jnp.sum(e, axis=-1, keepdims=True) + jnp.exp(sink - m))


def _alibi_slope(head):
    return 2.0 ** (-8.0 * (head + 1) / SWA_HEADS)


def _swa_prompt_kernel(sink_ref, q_ref, kp_ref, kc_ref, vp_ref, vc_ref, sg_ref, o_ref):
    n = pl.program_id(1)
    kk = jnp.concatenate([kp_ref[...], kc_ref[...]], axis=0).astype(BF16)
    vv = jnp.concatenate([vp_ref[...], vc_ref[...]], axis=0).astype(BF16)
    shape = (WINDOW, 2 * WINDOW)
    ki = lax.broadcasted_iota(jnp.int32, shape, 1)
    dist = WINDOW + lax.broadcasted_iota(jnp.int32, shape, 0) - ki
    valid = (dist >= 0) & (dist < WINDOW) & ((ki >= WINDOW) | (n > 0))
    distf = dist.astype(F32)
    q_low = lax.broadcasted_iota(jnp.int32, (WINDOW, LANES), 1) < SWA_HEAD_DIM
    kv_low = lax.broadcasted_iota(jnp.int32, (2 * WINDOW, LANES), 1) < SWA_HEAD_DIM
    zero = jnp.zeros((), BF16)
    for grp in range(SWA_HEADS // 2):
        pair = grp // SWA_GROUP
        lanes = slice(grp * LANES, (grp + 1) * LANES)
        kv_lanes = slice(pair * LANES, (pair + 1) * LANES)
        q_g = q_ref[:, lanes]
        k_g = kk[:, kv_lanes]
        v_g = vv[:, kv_lanes]
        o_g = jnp.zeros((WINDOW, LANES), F32)
        for par in range(2):
            head = SWA_HEAD_ORDER[2 * grp + par]
            keep_q = q_low if par == 0 else ~q_low
            keep_kv = kv_low if par == 0 else ~kv_low
            s = lax.dot_general(jnp.where(keep_q, q_g, zero), k_g, NT_DIMS,
                                preferred_element_type=F32) * SWA_SCALE
            s = s - _alibi_slope(head) * distf
            s = jnp.where(valid, s, NEG_INF)
            p = _sink_softmax(s, sink_ref[head])
            o_g = o_g + jnp.dot(p.astype(BF16), jnp.where(keep_kv, v_g, zero),
                                preferred_element_type=F32)
        o_ref[:, lanes] = (o_g * sg_ref[:, lanes]).astype(BF16)


def _swa_prompt(sinks, q, k, v, sg):
    b, s, _ = q.shape
    cur = lambda bi, n: (bi, n, 0)
    prev = lambda bi, n: (bi, jnp.maximum(n - 1, 0), 0)
    wide = lambda idx: pl.BlockSpec((None, WINDOW, SWA_Q), idx)
    narrow = lambda idx: pl.BlockSpec((None, WINDOW, SWA_KV), idx)
    return pl.pallas_call(
        _swa_prompt_kernel,
        grid=(b, s // WINDOW),
        in_specs=[pl.BlockSpec(memory_space=pltpu.SMEM),
                  wide(cur), narrow(prev), narrow(cur), narrow(prev), narrow(cur), wide(cur)],
        out_specs=wide(cur),
        out_shape=jax.ShapeDtypeStruct((b, s, SWA_Q), BF16),
        compiler_params=_params("parallel", "parallel"),
        name="swa_prompt",
    )(sinks, q, k, k, v, v, sg)


def _swa_sample_kernel(q_ref, bk_ref, bv_ref, kn_ref, vn_ref, sg_ref, sink_ref, slope_ref,
                       mask_ref, nk_ref, nv_ref, o_ref):
    nk_ref[0:WINDOW - 1, :] = bk_ref[1:WINDOW, :]
    nk_ref[WINDOW - 1:WINDOW, :] = kn_ref[...]
    nv_ref[0:WINDOW - 1, :] = bv_ref[1:WINDOW, :]
    nv_ref[WINDOW - 1:WINDOW, :] = vn_ref[...]
    kk = nk_ref[...].astype(BF16)
    vv = nv_ref[...].astype(BF16)
    mask = mask_ref[...] > 0.5
    q = q_ref[...]
    q_exp = jnp.where(mask, jnp.concatenate([q] * SWA_KV_HEADS, axis=-1), jnp.zeros((), BF16))
    s = lax.dot_general(q_exp, kk, NT_DIMS, preferred_element_type=F32) * SWA_SCALE
    dist = (WINDOW - 1 - lax.broadcasted_iota(jnp.int32, s.shape, 1)).astype(F32)
    s = s - slope_ref[...] * dist
    p = _sink_softmax(s, sink_ref[...])
    o_all = jnp.where(mask, jnp.dot(p.astype(BF16), vv, preferred_element_type=F32), 0.0)
    o = o_all[:, 0:SWA_HEAD_DIM]
    for kh in range(1, SWA_KV_HEADS):
        o = o + o_all[:, kh * SWA_HEAD_DIM:(kh + 1) * SWA_HEAD_DIM]
    o_ref[...] = (o * sg_ref[...]).astype(BF16)


def _swa_sample(q3, buf_k, buf_v, k_new, v_new, sg3, sink_col, slope_col, mask, layer):
    nb = q3.shape[0]
    per_b = lambda shape: pl.BlockSpec((None,) + shape, lambda b: (b, 0, 0))
    buf = pl.BlockSpec((None, None, WINDOW, SWA_KV), lambda b: (layer, b, 0, 0))
    return pl.pallas_call(
        _swa_sample_kernel,
        grid=(nb,),
        in_specs=[per_b((SWA_HEADS, SWA_HEAD_DIM)), buf, buf,
                  per_b((1, SWA_KV)), per_b((1, SWA_KV)), per_b((SWA_HEADS, SWA_HEAD_DIM)),
                  _full((SWA_HEADS, 1)), _full((SWA_HEADS, 1)), _full((SWA_HEADS, SWA_KV))],
        out_specs=(per_b((WINDOW, SWA_KV)), per_b((WINDOW, SWA_KV)),
                   per_b((SWA_HEADS, SWA_HEAD_DIM))),
        out_shape=(jax.ShapeDtypeStruct((nb, WINDOW, SWA_KV), F32),
                   jax.ShapeDtypeStruct((nb, WINDOW, SWA_KV), F32),
                   jax.ShapeDtypeStruct((nb, SWA_HEADS, SWA_HEAD_DIM), BF16)),
        compiler_params=_params("parallel"),
        name="swa_sample",
    )(q3, buf_k, buf_v, k_new, v_new, sg3, sink_col, slope_col, mask)


def _rot_cols(w):
    half = QK_ROPE // 2
    return jnp.concatenate([-w[..., half:], w[..., :half]], axis=-1)


def _rope_tables(pos):
    inv = ROPE_THETA ** (-jnp.arange(0, QK_ROPE, 2, dtype=F32) / QK_ROPE)
    ang = pos.astype(F32)[:, None] * inv[None, :]
    ang = jnp.concatenate([ang, ang], axis=-1)
    cos, sin = jnp.cos(ang), jnp.sin(ang)
    return (jnp.concatenate([cos, sin], axis=-1), jnp.concatenate([cos, cos], axis=-1),
            jnp.concatenate([sin, sin], axis=-1))


def _mla_weights(w_in, w_qb, w_uk, w_uv, w_o):
    g0 = Q_LORA + KV_LORA
    w_kpe = w_in[:, g0:g0 + QK_ROPE]
    w_in_r = jnp.concatenate(
        [w_in[:, :g0], w_in[:, g0 + QK_ROPE:], w_kpe, _rot_cols(w_kpe)], axis=1).astype(BF16)
    wqn = w_qb[:, :, :QK_NOPE].reshape(Q_LORA, MLA_HEADS * QK_NOPE).astype(BF16)
    wqr3 = w_qb[:, :, QK_NOPE:]
    wqr = wqr3.reshape(Q_LORA, MLA_HEADS * QK_ROPE).astype(BF16)
    wqrot = _rot_cols(wqr3).reshape(Q_LORA, MLA_HEADS * QK_ROPE).astype(BF16)
    wuk = w_uk.reshape(KV_LORA, MLA_HEADS * QK_NOPE).astype(BF16)
    wuv = w_uv.reshape(KV_LORA, MLA_HEADS * V_HEAD).astype(BF16)
    wukt = jnp.transpose(w_uk, (1, 2, 0)).astype(BF16)
    wuv3 = jnp.transpose(w_uv, (1, 0, 2)).astype(BF16)
    return w_in_r, wqn, wqr, wqrot, wuk, wuv, wukt, wuv3, w_o.astype(BF16)


def _swa_weights(w_in, w_o):
    cols = np.concatenate([np.arange(h * SWA_HEAD_DIM, (h + 1) * SWA_HEAD_DIM)
                           for h in SWA_HEAD_ORDER])
    g0 = SWA_Q + 2 * SWA_KV
    w_in_r = jnp.concatenate([w_in[:, :SWA_Q][:, cols], w_in[:, SWA_Q:g0], w_in[:, g0:][:, cols]],
                             axis=1).astype(BF16)
    return w_in_r, w_o[cols, :].astype(BF16)


def kernel(x_prompt, x_sample, cache_ckv, cache_kpe, state_swa_k, state_swa_v, page_table,
           pre_norm, post_norm, mla_w_in, mla_q_norm, mla_w_qb, mla_kv_norm, mla_w_uk,
           mla_w_uv, mla_w_o, swa_w_in, swa_sinks, swa_w_o):
    n_b, seq, _ = x_prompt.shape
    n_db, t_new, _ = x_sample.shape
    assert t_new == 1 and seq % 512 == 0 and state_swa_k.shape[2] == WINDOW
    depth = pre_norm.shape[0]
    past_len = page_table.shape[1] * PAGE_SIZE

    cs_p, cc_p, ss_p = _rope_tables(jnp.arange(seq, dtype=jnp.int32))
    cs_s, cc_s, ss_s = (jnp.broadcast_to(t, (n_db, LANES))
                        for t in _rope_tables(jnp.full((1,), past_len, jnp.int32)))

    order = np.asarray(SWA_HEAD_ORDER)
    slope_col = jnp.asarray([[_alibi_slope(h)] for h in SWA_HEAD_ORDER], F32)
    lane_kv = np.arange(SWA_KV)[None, :] // SWA_HEAD_DIM
    mask = jnp.asarray(lane_kv == (order // SWA_GROUP)[:, None], F32)
    buf_k = state_swa_k.reshape(state_swa_k.shape[:3] + (SWA_KV,))
    buf_v = state_swa_v.reshape(state_swa_v.shape[:3] + (SWA_KV,))

    hp = x_prompt.reshape(n_b * seq, D_MODEL)
    hs = x_sample.reshape(n_db, D_MODEL)
    ckv_p, kpe_p, ckv_s, kpe_s = [], [], [], []
    swk_p, swv_p, swk_s, swv_s = [], [], [], []
    for i in range(depth):
        l = i // 2
        g_pre = pre_norm[i][None, :]
        g_post = post_norm[i][None, :]
        if i % 2 == 0:
            w_in, wqn, wqr, wqrot, wuk, wuv, wukt, wuv3, w_o = _mla_weights(
                mla_w_in[l], mla_w_qb[l], mla_w_uk[l], mla_w_uv[l], mla_w_o[l])
            qn_g = mla_q_norm[l][None, :]
            kvn_g = mla_kv_norm[l][None, :]
            cq, ckv, ckvb, kpe2, kpeb, sg = _mla_in(hp, g_pre, w_in, qn_g, kvn_g, cs_p, 512)
            qn, qr2, kn, v = _mla_qkv(cq, ckvb, wqn, wqr, wqrot, wuk, wuv, cc_p, ss_p, 256)
            to3 = lambda a: a.reshape(n_b, seq, a.shape[-1])
            og = _mla_flash(to3(qn), to3(qr2), to3(kn), to3(kpeb), to3(v), to3(sg), 512)
            hp = _out_proj(og.reshape(n_b * seq, MLA_WIDTH), w_o, g_post, hp, 512)
            ckv_p.append(ckv.reshape(n_b, seq, KV_LORA))
            kpe_p.append(kpe2[:, :QK_ROPE].reshape(n_b, seq, QK_ROPE))
            cq, ckv, _, kpe2, _, sg = _mla_in(hs, g_pre, w_in, qn_g, kvn_g, cs_s, n_db)
            qn, qr = _mla_q(cq, wqn, wqr, wqrot, cc_s, ss_s)
            qlat = _qlat(qn, wukt)
            kpe_new = kpe2[:, :QK_ROPE]
            olat = _paged_attention(
                page_table, qlat.reshape(n_db, MLA_HEADS, KV_LORA),
                qr.reshape(n_db, MLA_HEADS, QK_ROPE), ckv.reshape(n_db, 1, KV_LORA),
                kpe_new.reshape(n_db, 1, QK_ROPE), cache_ckv, cache_kpe, l, 16)
            og = _olat_proj(olat.reshape(n_db, MLA_HEADS * KV_LORA), wuv3, sg)
            hs = _out_proj(og, w_o, g_post, hs, n_db)
            ckv_s.append(ckv.reshape(n_db, 1, KV_LORA))
            kpe_s.append(kpe_new.reshape(n_db, 1, QK_ROPE))
        else:
            w_in, w_o = _swa_weights(swa_w_in[l], swa_w_o[l])
            sinks = swa_sinks[l]
            q, k, v, sg = _swa_in(hp, g_pre, w_in, 512)
            to3 = lambda a: a.reshape(n_b, seq, a.shape[-1])
            og = _swa_prompt(sinks, to3(q), to3(k), to3(v), to3(sg))
            hp = _out_proj(og.reshape(n_b * seq, SWA_Q), w_o, g_post, hp, 512)
            keep = min(WINDOW, seq)
            tail = lambda a: to3(a)[:, seq - keep:].reshape(n_b, keep, SWA_KV_HEADS, SWA_HEAD_DIM)
            swk_p.append(tail(k))
            swv_p.append(tail(v))
            q, k, v, sg = _swa_in(hs, g_pre, w_in, n_db)
            nk, nv, og3 = _swa_sample(
                q.reshape(n_db, SWA_HEADS, SWA_HEAD_DIM), buf_k, buf_v,
                k.reshape(n_db, 1, SWA_KV), v.reshape(n_db, 1, SWA_KV),
                sg.reshape(n_db, SWA_HEADS, SWA_HEAD_DIM),
                sinks[order][:, None], slope_col, mask, l)
            hs = _out_proj(og3.reshape(n_db, SWA_Q), w_o, g_post, hs, n_db)
            swk_s.append(nk.reshape(n_db, WINDOW, SWA_KV_HEADS, SWA_HEAD_DIM))
            swv_s.append(nv.reshape(n_db, WINDOW, SWA_KV_HEADS, SWA_HEAD_DIM))
    return (hp.reshape(n_b, seq, D_MODEL), hs.reshape(n_db, t_new, D_MODEL),
            jnp.stack(ckv_p), jnp.stack(kpe_p), jnp.stack(ckv_s), jnp.stack(kpe_s),
            jnp.stack(swk_p), jnp.stack(swv_p), jnp.stack(swk_s), jnp.stack(swv_s))
```

```python
import functools
import math

import jax
import jax.numpy as jnp
import numpy as np
from jax import lax
from jax.experimental import pallas as pl
from jax.experimental.pallas import tpu as pltpu

F32 = jnp.float32
BF16 = jnp.bfloat16

D_MODEL = 1024
NORM_EPS = 1e-6
NEG_INF = -1e30
PAGE_SIZE = 128

MLA_HEADS = 16
Q_LORA = 512
KV_LORA = 256
QK_NOPE = 128
QK_ROPE = 64
V_HEAD = 128
ROPE_THETA = 10000.0
MLA_WIDTH = MLA_HEADS * V_HEAD
MLA_SCALE = 1.0 / math.sqrt(QK_NOPE + QK_ROPE)
MLA_QSCALE = MLA_SCALE * math.log2(math.e)

SWA_HEADS = 16
SWA_KV_HEADS = 4
SWA_GROUP = SWA_HEADS // SWA_KV_HEADS
SWA_HEAD_DIM = 64
WINDOW = 128
SWA_Q = SWA_HEADS * SWA_HEAD_DIM
SWA_KV = SWA_KV_HEADS * SWA_HEAD_DIM
SWA_SCALE = 1.0 / math.sqrt(SWA_HEAD_DIM)

LANES = 128
VMEM_LIMIT = 56 * 1024 * 1024

SWA_HEAD_ORDER = tuple(
    (2 * i + par) * SWA_GROUP + g
    for i in range(SWA_KV_HEADS // 2) for g in range(SWA_GROUP) for par in range(2))

NT_DIMS = (((1,), (1,)), ((), ()))


def _params(*sem):
    return pltpu.CompilerParams(dimension_semantics=sem, vmem_limit_bytes=VMEM_LIMIT)


def _rms(x, g):
    return x * lax.rsqrt(jnp.mean(x * x, axis=-1, keepdims=True) + NORM_EPS) * g


def _silu(x):
    return x * (1.0 / (1.0 + jnp.exp(-x)))


def _full(shape):
    return pl.BlockSpec(shape, lambda *_: (0,) * len(shape))


def _mla_in_kernel(h_ref, g_ref, w_ref, qn_ref, kvn_ref, cs_ref,
                   cq_ref, ckv_ref, ckvb_ref, kpe_ref, kpeb_ref, sg_ref):
    a = _rms(h_ref[...], g_ref[...]).astype(BF16)
    z = jnp.dot(a, w_ref[...], preferred_element_type=F32)
    cq_ref[...] = _rms(z[:, :Q_LORA], qn_ref[...]).astype(BF16)
    ckv = _rms(z[:, Q_LORA:Q_LORA + KV_LORA], kvn_ref[...])
    ckv_ref[...] = ckv
    ckvb_ref[...] = ckv.astype(BF16)
    g0 = Q_LORA + KV_LORA
    sg_ref[...] = _silu(z[:, g0:g0 + MLA_WIDTH])
    t = z[:, g0 + MLA_WIDTH:] * cs_ref[...]
    kpe2 = t + pltpu.roll(t, QK_ROPE, axis=1)
    kpe_ref[...] = kpe2
    kpeb_ref[...] = kpe2.astype(BF16)


def _mla_in(h, g, w, qn, kvn, cs, tm):
    rows = h.shape[0]
    n_cs = cs.shape[0] // tm
    n_in = w.shape[1]
    row = lambda r: (r, 0)
    outs = (
        jax.ShapeDtypeStruct((rows, Q_LORA), BF16),
        jax.ShapeDtypeStruct((rows, KV_LORA), F32),
        jax.ShapeDtypeStruct((rows, KV_LORA), BF16),
        jax.ShapeDtypeStruct((rows, LANES), F32),
        jax.ShapeDtypeStruct((rows, LANES), BF16),
        jax.ShapeDtypeStruct((rows, MLA_WIDTH), F32),
    )
    return pl.pallas_call(
        _mla_in_kernel,
        grid=(rows // tm,),
        in_specs=[
            pl.BlockSpec((tm, D_MODEL), row),
            _full((1, D_MODEL)),
            _full((D_MODEL, n_in)),
            _full((1, Q_LORA)),
            _full((1, KV_LORA)),
            pl.BlockSpec((tm, LANES), lambda r: (r % n_cs, 0)),
        ],
        out_specs=tuple(pl.BlockSpec((tm, o.shape[1]), row) for o in outs),
        out_shape=outs,
        compiler_params=_params("parallel"),
        name="mla_in",
    )(h, g, w, qn, kvn, cs)


def _rope_groups(cq, wr_ref, wrot_ref, cc, ss):
    r = jnp.dot(cq, wr_ref[...], preferred_element_type=F32)
    rr = jnp.dot(cq, wrot_ref[...], preferred_element_type=F32)
    n_grp = r.shape[1] // LANES
    return [r[:, j * LANES:(j + 1) * LANES] * cc + rr[:, j * LANES:(j + 1) * LANES] * ss
            for j in range(n_grp)]


def _mla_qkv_kernel(cq_ref, ckvb_ref, wqn_ref, wqr_ref, wqrot_ref, wuk_ref, wuvt_ref,
                    cc_ref, ss_ref, qn_ref, qr2_ref, kn_ref, vt_ref):
    cq = cq_ref[...]
    qn = jnp.dot(cq, wqn_ref[...], preferred_element_type=F32)
    qn_ref[...] = (qn * MLA_QSCALE).astype(BF16)
    groups = _rope_groups(cq, wqr_ref, wqrot_ref, cc_ref[...], ss_ref[...])
    low = lax.broadcasted_iota(jnp.int32, groups[0].shape, 1) < QK_ROPE
    for j, grp in enumerate(groups):
        grp = grp * MLA_QSCALE
        qr2_ref[:, (2 * j) * LANES:(2 * j + 1) * LANES] = jnp.where(low, grp, 0.0).astype(BF16)
        qr2_ref[:, (2 * j + 1) * LANES:(2 * j + 2) * LANES] = jnp.where(low, 0.0, grp).astype(BF16)
    ckv = ckvb_ref[...]
    kn_ref[...] = jnp.dot(ckv, wuk_ref[...], preferred_element_type=F32).astype(BF16)
    vt_ref[...] = lax.dot_general(wuvt_ref[...], ckv, NT_DIMS,
                                  preferred_element_type=F32).astype(BF16)


def _mla_qkv(cq, ckvb, wqn, wqr, wqrot, wuk, wuvt, cc, ss, tm):
    rows = cq.shape[0]
    n_cs = cc.shape[0] // tm
    row = lambda r: (r, 0)
    tab = pl.BlockSpec((tm, LANES), lambda r: (r % n_cs, 0))
    wide = jax.ShapeDtypeStruct((rows, MLA_WIDTH), BF16)
    return pl.pallas_call(
        _mla_qkv_kernel,
        grid=(rows // tm,),
        in_specs=[
            pl.BlockSpec((tm, Q_LORA), row),
            pl.BlockSpec((tm, KV_LORA), row),
            _full(wqn.shape), _full(wqr.shape), _full(wqrot.shape),
            _full(wuk.shape), _full(wuvt.shape), tab, tab,
        ],
        out_specs=(pl.BlockSpec((tm, MLA_WIDTH), row),) * 3
        + (pl.BlockSpec((MLA_WIDTH, tm), lambda r: (0, r)),),
        out_shape=(wide,) * 3 + (jax.ShapeDtypeStruct((MLA_WIDTH, rows), BF16),),
        compiler_params=_params("parallel"),
        name="mla_qkv",
    )(cq, ckvb, wqn, wqr, wqrot, wuk, wuvt, cc, ss)


def _mla_q_kernel(cq_ref, wqn_ref, wqr_ref, wqrot_ref, cc_ref, ss_ref, qn_ref, qr_ref):
    cq = cq_ref[...]
    qn_ref[...] = jnp.dot(cq, wqn_ref[...], preferred_element_type=F32).astype(BF16)
    groups = _rope_groups(cq, wqr_ref, wqrot_ref, cc_ref[...], ss_ref[...])
    for j, grp in enumerate(groups):
        qr_ref[:, j * LANES:(j + 1) * LANES] = grp.astype(BF16)


def _mla_q(cq, wqn, wqr, wqrot, cc, ss):
    rows = cq.shape[0]
    return pl.pallas_call(
        _mla_q_kernel,
        grid=(1,),
        in_specs=[_full(cq.shape), _full(wqn.shape), _full(wqr.shape), _full(wqrot.shape),
                  _full(cc.shape), _full(ss.shape)],
        out_specs=(_full((rows, MLA_WIDTH)), _full((rows, MLA_HEADS * QK_ROPE))),
        out_shape=(jax.ShapeDtypeStruct((rows, MLA_WIDTH), BF16),
                   jax.ShapeDtypeStruct((rows, MLA_HEADS * QK_ROPE), BF16)),
        compiler_params=_params("arbitrary"),
        name="mla_q_sample",
    )(cq, wqn, wqr, wqrot, cc, ss)


def _mla_flash_kernel(qn_ref, qr_ref, kn_ref, kpe_ref, vt_ref, sg_ref, o_ref, s_scr, *,
                      blk, sub, heads):
    i = pl.program_id(2)
    lanes = [slice(h * LANES, (h + 1) * LANES) for h in range(heads)]
    q = [jnp.concatenate([qn_ref[:, ln], qr_ref[:, ln]], axis=-1) for ln in lanes]

    def scores(h, start, col0, slot):
        rows = pl.ds(start, sub)
        k = jnp.concatenate([kn_ref[rows, lanes[h]], kpe_ref[rows, :]], axis=-1)
        s_scr[h, slot, :, col0:] = lax.dot_general(k, q[h][col0:, :], NT_DIMS,
                                                   preferred_element_type=F32)

    def fold(h, slot, start, col0, carry, diagonal):
        m, l, acc = (c[:, col0:] for c in carry)
        rows = pl.ds(start, sub)
        s = s_scr[h, slot, :, col0:]
        if diagonal:
            ki = lax.broadcasted_iota(jnp.int32, s.shape, 0)
            qi = lax.broadcasted_iota(jnp.int32, s.shape, 1)
            s = jnp.where(ki <= qi, s, NEG_INF)
        m_new = jnp.maximum(m, jnp.max(s, axis=0, keepdims=True))
        alpha = jnp.exp2(m - m_new)
        p = jnp.exp2(s - m_new)
        l = alpha * l + jnp.sum(p, axis=0, keepdims=True)
        acc = alpha * acc + jnp.dot(vt_ref[lanes[h], rows], p.astype(BF16),
                                    preferred_element_type=F32)
        if col0 == 0:
            return m_new, l, acc
        return tuple(jnp.concatenate([c[:, :col0], n], axis=1)
                     for c, n in zip(carry, (m_new, l, acc)))

    n_sub = blk // sub

    def full_block(j, carries):
        base = pl.multiple_of(j * blk, blk)
        carries = list(carries)
        for u in range(n_sub):
            for h in range(heads):
                scores(h, base + (u + 1) * sub, 0, (u + 1) % 2)
            for h in range(heads):
                carries[h] = fold(h, u % 2, base + u * sub, 0, carries[h], False)
        return tuple(carries)

    init = (jnp.full((1, blk), NEG_INF, F32), jnp.zeros((1, blk), F32),
            jnp.zeros((V_HEAD, blk), F32))
    for h in range(heads):
        scores(h, 0, 0, 0)
    carries = list(lax.fori_loop(0, i, full_block, (init,) * heads))
    base = pl.multiple_of(i * blk, blk)
    for u in range(n_sub):
        if u + 1 < n_sub:
            for h in range(heads):
                scores(h, base + (u + 1) * sub, (u + 1) * sub, (u + 1) % 2)
        for h in range(heads):
            carries[h] = fold(h, u % 2, base + u * sub, u * sub, carries[h], True)
    for h in range(heads):
        _, l, acc = carries[h]
        o_ref[:, lanes[h]] = ((acc / l).T * sg_ref[:, lanes[h]]).astype(BF16)


def _mla_flash(qn, qr2, kn, kpeb, vt, sg, blk, sub, heads):
    b, s, _ = qn.shape
    width = heads * LANES
    q_spec = pl.BlockSpec((None, blk, width), lambda bi, h, i: (bi, i, h))
    return pl.pallas_call(
        functools.partial(_mla_flash_kernel, blk=blk, sub=sub, heads=heads),
        grid=(b, MLA_HEADS // heads, s // blk),
        in_specs=[q_spec, q_spec,
                  pl.BlockSpec((None, s, width), lambda bi, h, i: (bi, 0, h)),
                  pl.BlockSpec((None, s, LANES), lambda bi, h, i: (bi, 0, 0)),
                  pl.BlockSpec((heads * V_HEAD, s), lambda bi, h, i: (h, bi)),
                  q_spec],
        out_specs=q_spec,
        out_shape=jax.ShapeDtypeStruct((b, s, MLA_WIDTH), BF16),
        scratch_shapes=[pltpu.VMEM((heads, 2, sub, blk), F32)],
        compiler_params=_params("parallel", "parallel", "arbitrary"),
        name="mla_flash",
    )(qn, qr2, kn, kpeb, vt, sg)


def _out_kernel(x_ref, w_ref, g_ref, h_ref, o_ref):
    y = jnp.dot(x_ref[...], w_ref[...], preferred_element_type=F32)
    o_ref[...] = h_ref[...] + _rms(y, g_ref[...])


def _out_proj(x, w, g, h, tm):
    rows, width = x.shape
    row = lambda r: (r, 0)
    return pl.pallas_call(
        _out_kernel,
        grid=(rows // tm,),
        in_specs=[pl.BlockSpec((tm, width), row), _full(w.shape), _full((1, D_MODEL)),
                  pl.BlockSpec((tm, D_MODEL), row)],
        out_specs=pl.BlockSpec((tm, D_MODEL), row),
        out_shape=jax.ShapeDtypeStruct((rows, D_MODEL), F32),
        compiler_params=_params("parallel"),
        name="out_proj",
    )(x, w, g, h)


def _qlat_kernel(qn_ref, wukt_ref, o_ref):
    o_ref[...] = jnp.dot(qn_ref[...], wukt_ref[...], preferred_element_type=F32).astype(BF16)


def _qlat(qn, wukt):
    rows = qn.shape[0]
    return pl.pallas_call(
        _qlat_kernel,
        grid=(MLA_HEADS,),
        in_specs=[pl.BlockSpec((rows, QK_NOPE), lambda h: (0, h)),
                  pl.BlockSpec((None, QK_NOPE, KV_LORA), lambda h: (h, 0, 0))],
        out_specs=pl.BlockSpec((rows, KV_LORA), lambda h: (0, h)),
        out_shape=jax.ShapeDtypeStruct((rows, MLA_HEADS * KV_LORA), BF16),
        compiler_params=_params("parallel"),
        name="mla_qlat",
    )(qn, wukt)


def _paged_kernel(pt_ref, ql_ref, qr_ref, cnew_ref, knew_ref, *rest, n_pages):
    ckv_refs = rest[:n_pages]
    kpe_refs = rest[n_pages:2 * n_pages]
    o_ref, kbuf, pbuf, m_sc, l_sc, acc_sc = rest[2 * n_pages:]
    c = pl.program_id(1)
    ql = ql_ref[...]
    qr = qr_ref[...]

    @pl.when(c == 0)
    def _():
        cn = cnew_ref[...].astype(BF16).astype(F32)
        kn = knew_ref[...].astype(BF16).astype(F32)
        s_new = (jnp.sum(ql.astype(F32) * cn, axis=-1, keepdims=True)
                 + jnp.sum(qr.astype(F32) * kn, axis=-1, keepdims=True)) * MLA_SCALE
        m_sc[...] = s_new
        l_sc[...] = jnp.ones_like(s_new)
        acc_sc[...] = jnp.broadcast_to(cn, acc_sc.shape)

    for p in range(n_pages):
        kbuf[p * PAGE_SIZE:(p + 1) * PAGE_SIZE, :] = ckv_refs[p][...].astype(BF16)
        pbuf[:, p * PAGE_SIZE:(p + 1) * PAGE_SIZE] = kpe_refs[p][...].astype(BF16)
    kk = kbuf[...]
    s = (lax.dot_general(ql, kk, NT_DIMS, preferred_element_type=F32)
         + jnp.dot(qr, pbuf[...], preferred_element_type=F32)) * MLA_SCALE
    m_old = m_sc[...]
    m_new = jnp.maximum(m_old, jnp.max(s, axis=-1, keepdims=True))
    alpha = jnp.exp(m_old - m_new)
    p_ = jnp.exp(s - m_new)
    l_new = alpha * l_sc[...] + jnp.sum(p_, axis=-1, keepdims=True)
    acc = alpha * acc_sc[...] + jnp.dot(p_.astype(BF16), kk, preferred_element_type=F32)
    m_sc[...] = m_new
    l_sc[...] = l_new
    acc_sc[...] = acc

    @pl.when(c == pl.num_programs(1) - 1)
    def _():
        o_ref[...] = acc / l_new


def _paged_attention(page_table, qlat3, qr3, ckv_new, kpe_new, cache_ckv, cache_kpe_t, layer,
                     n_pages):
    nb, pages_per_seq = page_table.shape
    n_chunks = pages_per_seq // n_pages

    def page_spec(shape, p):
        return pl.BlockSpec((None, None) + shape,
                            lambda b, c, pt: (layer, pt[b, c * n_pages + p], 0, 0))

    per_b = lambda shape: pl.BlockSpec((None,) + shape, lambda b, c, pt: (b, 0, 0))
    grid_spec = pltpu.PrefetchScalarGridSpec(
        num_scalar_prefetch=1,
        grid=(nb, n_chunks),
        in_specs=[per_b((MLA_HEADS, KV_LORA)), per_b((MLA_HEADS, QK_ROPE)),
                  per_b((1, KV_LORA)), per_b((1, QK_ROPE))]
        + [page_spec((PAGE_SIZE, KV_LORA), p) for p in range(n_pages)]
        + [page_spec((QK_ROPE, PAGE_SIZE), p) for p in range(n_pages)],
        out_specs=per_b((MLA_HEADS, KV_LORA)),
        scratch_shapes=[
            pltpu.VMEM((n_pages * PAGE_SIZE, KV_LORA), BF16),
            pltpu.VMEM((QK_ROPE, n_pages * PAGE_SIZE), BF16),
            pltpu.VMEM((MLA_HEADS, 1), F32),
            pltpu.VMEM((MLA_HEADS, 1), F32),
            pltpu.VMEM((MLA_HEADS, KV_LORA), F32),
        ],
    )
    return pl.pallas_call(
        functools.partial(_paged_kernel, n_pages=n_pages),
        grid_spec=grid_spec,
        out_shape=jax.ShapeDtypeStruct((nb, MLA_HEADS, KV_LORA), F32),
        compiler_params=_params("parallel", "arbitrary"),
        name="mla_paged",
    )(page_table, qlat3, qr3, ckv_new, kpe_new,
      *([cache_ckv] * n_pages), *([cache_kpe_t] * n_pages))


def _olat_kernel(ol_ref, wuv_ref, sg_ref, o_ref):
    o = jnp.dot(ol_ref[...].astype(BF16), wuv_ref[...], preferred_element_type=F32)
    o_ref[...] = (o * sg_ref[...]).astype(BF16)


def _olat_proj(olat, wuv3, sg):
    rows = olat.shape[0]
    return pl.pallas_call(
        _olat_kernel,
        grid=(MLA_HEADS,),
        in_specs=[pl.BlockSpec((rows, KV_LORA), lambda h: (0, h)),
                  pl.BlockSpec((None, KV_LORA, V_HEAD), lambda h: (h, 0, 0)),
                  pl.BlockSpec((rows, V_HEAD), lambda h: (0, h))],
        out_specs=pl.BlockSpec((rows, V_HEAD), lambda h: (0, h)),
        out_shape=jax.ShapeDtypeStruct((rows, MLA_WIDTH), BF16),
        compiler_params=_params("parallel"),
        name="mla_olat",
    )(olat, wuv3, sg)


def _swa_in_kernel(h_ref, g_ref, w_ref, q_ref, k_ref, v_ref, sg_ref):
    a = _rms(h_ref[...], g_ref[...]).astype(BF16)
    z = jnp.dot(a, w_ref[...], preferred_element_type=F32)
    q_ref[...] = (z[:, :SWA_Q] * SWA_SCALE).astype(BF16)
    k_ref[...] = z[:, SWA_Q:SWA_Q + SWA_KV]
    v_ref[...] = z[:, SWA_Q + SWA_KV:SWA_Q + 2 * SWA_KV]
    sg_ref[...] = _silu(z[:, SWA_Q + 2 * SWA_KV:])


def _swa_in(h, g, w, tm):
    rows = h.shape[0]
    row = lambda r: (r, 0)
    outs = (jax.ShapeDtypeStruct((rows, SWA_Q), BF16),
            jax.ShapeDtypeStruct((rows, SWA_KV), F32),
            jax.ShapeDtypeStruct((rows, SWA_KV), F32),
            jax.ShapeDtypeStruct((rows, SWA_Q), F32))
    return pl.pallas_call(
        _swa_in_kernel,
        grid=(rows // tm,),
        in_specs=[pl.BlockSpec((tm, D_MODEL), row), _full((1, D_MODEL)), _full(w.shape)],
        out_specs=tuple(pl.BlockSpec((tm, o.shape[1]), row) for o in outs),
        out_shape=outs,
        compiler_params=_params("parallel"),
        name="swa_in",
    )(h, g, w)


def _sink_softmax(s, sink):
    m = jnp.maximum(jnp.max(s, axis=-1, keepdims=True), sink)
    e = jnp.exp(s - m)
    return e / (jnp.sum(e, axis=-1, keepdims=True) + jnp.exp(sink - m))


def _alibi_slope(head):
    return 2.0 ** (-8.0 * (head + 1) / SWA_HEADS)


def _swa_bias():
    dist = (WINDOW + np.arange(WINDOW))[:, None] - np.arange(2 * WINDOW)[None, :]
    in_window = (dist >= 0) & (dist < WINDOW)
    slopes = np.asarray([_alibi_slope(h) for h in range(SWA_HEADS)], np.float32)
    bias = -slopes[:, None, None] * dist[None].astype(np.float32)
    return np.where(in_window[None], bias, np.float32(NEG_INF)).astype(np.float32)


def _swa_prompt_kernel(sink_ref, q_ref, kp_ref, kc_ref, vp_ref, vc_ref, sg_ref, bias_ref, o_ref):
    n = pl.program_id(1)
    kk = jnp.concatenate([kp_ref[...], kc_ref[...]], axis=0).astype(BF16)
    vv = jnp.concatenate([vp_ref[...], vc_ref[...]], axis=0).astype(BF16)
    kv_low = lax.broadcasted_iota(jnp.int32, (2 * WINDOW, LANES), 1) < SWA_HEAD_DIM
    zero = jnp.zeros((), BF16)
    prev_pen = jnp.where(n > 0, 0.0, NEG_INF).astype(F32)
    rows = lambda g: slice(g * WINDOW, (g + 1) * WINDOW)
    for pair in range(SWA_KV_HEADS // 2):
        kv_lanes = slice(pair * LANES, (pair + 1) * LANES)
        k_g = kk[:, kv_lanes]
        v_g = vv[:, kv_lanes]
        grp_lanes = [slice((pair * SWA_GROUP + g) * LANES, (pair * SWA_GROUP + g + 1) * LANES)
                     for g in range(SWA_GROUP)]
        q_st = jnp.concatenate([q_ref[:, lanes] for lanes in grp_lanes], axis=0)
        o_pair = jnp.zeros((SWA_GROUP * WINDOW, LANES), F32)
        for par in range(2):
            keep = kv_low if par == 0 else ~kv_low
            s_all = lax.dot_general(q_st, jnp.where(keep, k_g, zero), NT_DIMS,
                                    preferred_element_type=F32)
            es, inv_dens = [], []
            for g in range(SWA_GROUP):
                head = SWA_HEAD_ORDER[2 * (pair * SWA_GROUP + g) + par]
                s = s_all[rows(g), :] + bias_ref[head]
                s_prev = s[:, :WINDOW] + prev_pen
                s_cur = s[:, WINDOW:]
                sink = sink_ref[head]
                m = jnp.maximum(jnp.maximum(jnp.max(s_prev, axis=-1, keepdims=True),
                                            jnp.max(s_cur, axis=-1, keepdims=True)), sink)
                e_prev = jnp.exp(s_prev - m)
                e_cur = jnp.exp(s_cur - m)
                den = (jnp.sum(e_prev, axis=-1, keepdims=True)
                       + jnp.sum(e_cur, axis=-1, keepdims=True) + jnp.exp(sink - m))
                es.append(jnp.concatenate([e_prev, e_cur], axis=-1).astype(BF16))
                inv_dens.append(1.0 / den)
            o_par = jnp.dot(jnp.concatenate(es, axis=0), jnp.where(keep, v_g, zero),
                            preferred_element_type=F32)
            o_pair = o_pair + o_par * jnp.concatenate(inv_dens, axis=0)
        for g, lanes in enumerate(grp_lanes):
            o_ref[:, lanes] = (o_pair[rows(g), :] * sg_ref[:, lanes]).astype(BF16)


def _swa_prompt(sinks, q, k, v, sg):
    b, s, _ = q.shape
    cur = lambda bi, n: (bi, n, 0)
    prev = lambda bi, n: (bi, jnp.maximum(n - 1, 0), 0)
    wide = lambda idx: pl.BlockSpec((None, WINDOW, SWA_Q), idx)
    narrow = lambda idx: pl.BlockSpec((None, WINDOW, SWA_KV), idx)
    bias = _swa_bias()
    return pl.pallas_call(
        _swa_prompt_kernel,
        grid=(b, s // WINDOW),
        in_specs=[pl.BlockSpec(memory_space=pltpu.SMEM),
                  wide(cur), narrow(prev), narrow(cur), narrow(prev), narrow(cur), wide(cur),
                  _full(bias.shape)],
        out_specs=wide(cur),
        out_shape=jax.ShapeDtypeStruct((b, s, SWA_Q), BF16),
        compiler_params=_params("parallel", "parallel"),
        name="swa_prompt",
    )(sinks, q, k, k, v, v, sg, jnp.asarray(bias))


def _swa_sample_kernel(q_ref, bk_ref, bv_ref, kn_ref, vn_ref, sg_ref, sink_ref, slope_ref,
                       mask_ref, nk_ref, nv_ref, o_ref):
    nk_ref[0:WINDOW - 1, :] = bk_ref[1:WINDOW, :]
    nk_ref[WINDOW - 1:WINDOW, :] = kn_ref[...]
    nv_ref[0:WINDOW - 1, :] = bv_ref[1:WINDOW, :]
    nv_ref[WINDOW - 1:WINDOW, :] = vn_ref[...]
    kk = nk_ref[...].astype(BF16)
    vv = nv_ref[...].astype(BF16)
    mask = mask_ref[...] > 0.5
    q = q_ref[...]
    q_exp = jnp.where(mask, jnp.concatenate([q] * SWA_KV_HEADS, axis=-1), jnp.zeros((), BF16))
    s = lax.dot_general(q_exp, kk, NT_DIMS, preferred_element_type=F32)
    dist = (WINDOW - 1 - lax.broadcasted_iota(jnp.int32, s.shape, 1)).astype(F32)
    s = s - slope_ref[...] * dist
    p = _sink_softmax(s, sink_ref[...])
    o_all = jnp.where(mask, jnp.dot(p.astype(BF16), vv, preferred_element_type=F32), 0.0)
    o = o_all[:, 0:SWA_HEAD_DIM]
    for kh in range(1, SWA_KV_HEADS):
        o = o + o_all[:, kh * SWA_HEAD_DIM:(kh + 1) * SWA_HEAD_DIM]
    o_ref[...] = (o * sg_ref[...]).astype(BF16)


def _swa_sample(q3, buf_k, buf_v, k_new, v_new, sg3, sink_col, slope_col, mask, layer):
    nb = q3.shape[0]
    per_b = lambda shape: pl.BlockSpec((None,) + shape, lambda b: (b, 0, 0))
    buf = pl.BlockSpec((None, None, WINDOW, SWA_KV), lambda b: (layer, b, 0, 0))
    return pl.pallas_call(
        _swa_sample_kernel,
        grid=(nb,),
        in_specs=[per_b((SWA_HEADS, SWA_HEAD_DIM)), buf, buf,
                  per_b((1, SWA_KV)), per_b((1, SWA_KV)), per_b((SWA_HEADS, SWA_HEAD_DIM)),
                  _full((SWA_HEADS, 1)), _full((SWA_HEADS, 1)), _full((SWA_HEADS, SWA_KV))],
        out_specs=(per_b((WINDOW, SWA_KV)), per_b((WINDOW, SWA_KV)),
                   per_b((SWA_HEADS, SWA_HEAD_DIM))),
        out_shape=(jax.ShapeDtypeStruct((nb, WINDOW, SWA_KV), F32),
                   jax.ShapeDtypeStruct((nb, WINDOW, SWA_KV), F32),
                   jax.ShapeDtypeStruct((nb, SWA_HEADS, SWA_HEAD_DIM), BF16)),
        compiler_params=_params("parallel"),
        name="swa_sample",
    )(q3, buf_k, buf_v, k_new, v_new, sg3, sink_col, slope_col, mask)


def _rot_cols(w):
    half = QK_ROPE // 2
    return jnp.concatenate([-w[..., half:], w[..., :half]], axis=-1)


def _rope_tables(pos):
    inv = ROPE_THETA ** (-jnp.arange(0, QK_ROPE, 2, dtype=F32) / QK_ROPE)
    ang = pos.astype(F32)[:, None] * inv[None, :]
    ang = jnp.concatenate([ang, ang], axis=-1)
    cos, sin = jnp.cos(ang), jnp.sin(ang)
    return (jnp.concatenate([cos, sin], axis=-1), jnp.concatenate([cos, cos], axis=-1),
            jnp.concatenate([sin, sin], axis=-1))


def _mla_weights(w_in, w_qb, w_uk, w_uv, w_o):
    g0 = Q_LORA + KV_LORA
    w_kpe = w_in[:, g0:g0 + QK_ROPE]
    w_in_r = jnp.concatenate(
        [w_in[:, :g0], w_in[:, g0 + QK_ROPE:], w_kpe, _rot_cols(w_kpe)], axis=1).astype(BF16)
    wqn = w_qb[:, :, :QK_NOPE].reshape(Q_LORA, MLA_HEADS * QK_NOPE).astype(BF16)
    wqr3 = w_qb[:, :, QK_NOPE:]
    wqr = wqr3.reshape(Q_LORA, MLA_HEADS * QK_ROPE).astype(BF16)
    wqrot = _rot_cols(wqr3).reshape(Q_LORA, MLA_HEADS * QK_ROPE).astype(BF16)
    wuk = w_uk.reshape(KV_LORA, MLA_HEADS * QK_NOPE).astype(BF16)
    wuvt = w_uv.reshape(KV_LORA, MLA_HEADS * V_HEAD).T.astype(BF16)
    wukt = jnp.transpose(w_uk, (1, 2, 0)).astype(BF16)
    wuv3 = jnp.transpose(w_uv, (1, 0, 2)).astype(BF16)
    return w_in_r, wqn, wqr, wqrot, wuk, wuvt, wukt, wuv3, w_o.astype(BF16)


def _swa_weights(w_in, w_o):
    cols = np.concatenate([np.arange(h * SWA_HEAD_DIM, (h + 1) * SWA_HEAD_DIM)
                           for h in SWA_HEAD_ORDER])
    g0 = SWA_Q + 2 * SWA_KV
    w_in_r = jnp.concatenate([w_in[:, :SWA_Q][:, cols], w_in[:, SWA_Q:g0], w_in[:, g0:][:, cols]],
                             axis=1).astype(BF16)
    return w_in_r, w_o[cols, :].astype(BF16)


def kernel(x_prompt, x_sample, cache_ckv, cache_kpe, state_swa_k, state_swa_v, page_table,
           pre_norm, post_norm, mla_w_in, mla_q_norm, mla_w_qb, mla_kv_norm, mla_w_uk,
           mla_w_uv, mla_w_o, swa_w_in, swa_sinks, swa_w_o):
    n_b, seq, _ = x_prompt.shape
    n_db, t_new, _ = x_sample.shape
    assert t_new == 1 and seq % 512 == 0 and state_swa_k.shape[2] == WINDOW
    depth = pre_norm.shape[0]
    past_len = page_table.shape[1] * PAGE_SIZE

    cs_p, cc_p, ss_p = _rope_tables(jnp.arange(seq, dtype=jnp.int32))
    cs_s, cc_s, ss_s = (jnp.broadcast_to(t, (n_db, LANES))
                        for t in _rope_tables(jnp.full((1,), past_len, jnp.int32)))

    order = np.asarray(SWA_HEAD_ORDER)
    slope_col = jnp.asarray([[_alibi_slope(h)] for h in SWA_HEAD_ORDER], F32)
    lane_kv = np.arange(SWA_KV)[None, :] // SWA_HEAD_DIM
    mask = jnp.asarray(lane_kv == (order // SWA_GROUP)[:, None], F32)
    cache_kpe_t = jnp.swapaxes(cache_kpe, 2, 3)
    buf_k = state_swa_k.reshape(state_swa_k.shape[:3] + (SWA_KV,))
    buf_v = state_swa_v.reshape(state_swa_v.shape[:3] + (SWA_KV,))

    hp = x_prompt.reshape(n_b * seq, D_MODEL)
    hs = x_sample.reshape(n_db, D_MODEL)
    ckv_p, kpe_p, ckv_s, kpe_s = [], [], [], []
    swk_p, swv_p, swk_s, swv_s = [], [], [], []
    for i in range(depth):
        l = i // 2
        g_pre = pre_norm[i][None, :]
        g_post = post_norm[i][None, :]
        if i % 2 == 0:
            w_in, wqn, wqr, wqrot, wuk, wuvt, wukt, wuv3, w_o = _mla_weights(
                mla_w_in[l], mla_w_qb[l], mla_w_uk[l], mla_w_uv[l], mla_w_o[l])
            qn_g = mla_q_norm[l][None, :]
            kvn_g = mla_kv_norm[l][None, :]
            cq, ckv, ckvb, kpe2, kpeb, sg = _mla_in(hp, g_pre, w_in, qn_g, kvn_g, cs_p, 512)
            qn, qr2, kn, vt = _mla_qkv(cq, ckvb, wqn, wqr, wqrot, wuk, wuvt, cc_p, ss_p, 256)
            to3 = lambda a: a.reshape(n_b, seq, a.shape[-1])
            og = _mla_flash(to3(qn), to3(qr2), to3(kn), to3(kpeb), vt, to3(sg), 512, 256, 2)
            hp = _out_proj(og.reshape(n_b * seq, MLA_WIDTH), w_o, g_post, hp, 512)
            ckv_p.append(ckv.reshape(n_b, seq, KV_LORA))
            kpe_p.append(kpe2[:, :QK_ROPE].reshape(n_b, seq, QK_ROPE))
            cq, ckv, _, kpe2, _, sg = _mla_in(hs, g_pre, w_in, qn_g, kvn_g, cs_s, n_db)
            qn, qr = _mla_q(cq, wqn, wqr, wqrot, cc_s, ss_s)
            qlat = _qlat(qn, wukt)
            kpe_new = kpe2[:, :QK_ROPE]
            olat = _paged_attention(
                page_table, qlat.reshape(n_db, MLA_HEADS, KV_LORA),
                qr.reshape(n_db, MLA_HEADS, QK_ROPE), ckv.reshape(n_db, 1, KV_LORA),
                kpe_new.reshape(n_db, 1, QK_ROPE), cache_ckv, cache_kpe_t, l, 16)
            og = _olat_proj(olat.reshape(n_db, MLA_HEADS * KV_LORA), wuv3, sg)
            hs = _out_proj(og, w_o, g_post, hs, n_db)
            ckv_s.append(ckv.reshape(n_db, 1, KV_LORA))
            kpe_s.append(kpe_new.reshape(n_db, 1, QK_ROPE))
        else:
            w_in, w_o = _swa_weights(swa_w_in[l], swa_w_o[l])
            sinks = swa_sinks[l]
            q, k, v, sg = _swa_in(hp, g_pre, w_in, 512)
            to3 = lambda a: a.reshape(n_b, seq, a.shape[-1])
            og = _swa_prompt(sinks, to3(q), to3(k), to3(v), to3(sg))
            hp = _out_proj(og.reshape(n_b * seq, SWA_Q), w_o, g_post, hp, 512)
            keep = min(WINDOW, seq)
            tail = lambda a: to3(a)[:, seq - keep:].reshape(n_b, keep, SWA_KV_HEADS, SWA_HEAD_DIM)
            swk_p.append(tail(k))
            swv_p.append(tail(v))
            q, k, v, sg = _swa_in(hs, g_pre, w_in, n_db)
            nk, nv, og3 = _swa_sample(
                q.reshape(n_db, SWA_HEADS, SWA_HEAD_DIM), buf_k, buf_v,
                k.reshape(n_db, 1, SWA_KV), v.reshape(n_db, 1, SWA_KV),
                sg.reshape(n_db, SWA_HEADS, SWA_HEAD_DIM),
                sinks[order][:, None], slope_col, mask, l)
            hs = _out_proj(og3.reshape(n_db, SWA_Q), w_o, g_post, hs, n_db)
            swk_s.append(nk.reshape(n_db, WINDOW, SWA_KV_HEADS, SWA_HEAD_DIM))
            swv_s.append(nv.reshape(n_db, WINDOW, SWA_KV_HEADS, SWA_HEAD_DIM))
    return (hp.reshape(n_b, seq, D_MODEL), hs.reshape(n_db, t_new, D_MODEL),
            jnp.stack(ckv_p), jnp.stack(kpe_p), jnp.stack(ckv_s), jnp.stack(kpe_s),
            jnp.stack(swk_p), jnp.stack(swv_p), jnp.stack(swk_s), jnp.stack(swv_s))
```

```python
import functools
import math

import jax
import jax.numpy as jnp
import numpy as np
from jax import lax
from jax.experimental import pallas as pl
from jax.experimental.pallas import tpu as pltpu

F32 = jnp.float32
BF16 = jnp.bfloat16

D_MODEL = 1024
NORM_EPS = 1e-6
NEG_INF = -1e30
PAGE_SIZE = 128

MLA_HEADS = 16
Q_LORA = 512
KV_LORA = 256
QK_NOPE = 128
QK_ROPE = 64
V_HEAD = 128
ROPE_THETA = 10000.0
MLA_WIDTH = MLA_HEADS * V_HEAD
MLA_SCALE = 1.0 / math.sqrt(QK_NOPE + QK_ROPE)
MLA_QSCALE = MLA_SCALE * math.log2(math.e)

SWA_HEADS = 16
SWA_KV_HEADS = 4
SWA_GROUP = SWA_HEADS // SWA_KV_HEADS
SWA_HEAD_DIM = 64
WINDOW = 128
SWA_Q = SWA_HEADS * SWA_HEAD_DIM
SWA_KV = SWA_KV_HEADS * SWA_HEAD_DIM
SWA_SCALE = 1.0 / math.sqrt(SWA_HEAD_DIM)

LANES = 128
VMEM_LIMIT = 56 * 1024 * 1024

SWA_HEAD_ORDER = tuple(
    (2 * i + par) * SWA_GROUP + g
    for i in range(SWA_KV_HEADS // 2) for g in range(SWA_GROUP) for par in range(2))

NT_DIMS = (((1,), (1,)), ((), ()))


def _params(*sem):
    return pltpu.CompilerParams(dimension_semantics=sem, vmem_limit_bytes=VMEM_LIMIT)


def _rms(x, g):
    return x * lax.rsqrt(jnp.mean(x * x, axis=-1, keepdims=True) + NORM_EPS) * g


def _silu(x):
    return x * (1.0 / (1.0 + jnp.exp(-x)))


def _full(shape):
    return pl.BlockSpec(shape, lambda *_: (0,) * len(shape))


def _mla_in_kernel(h_ref, g_ref, w_ref, qn_ref, kvn_ref, cs_ref,
                   cq_ref, ckv_ref, ckvb_ref, kpe_ref, kpeb_ref, sg_ref):
    a = _rms(h_ref[...], g_ref[...]).astype(BF16)
    z = jnp.dot(a, w_ref[...], preferred_element_type=F32)
    cq_ref[...] = _rms(z[:, :Q_LORA], qn_ref[...]).astype(BF16)
    ckv = _rms(z[:, Q_LORA:Q_LORA + KV_LORA], kvn_ref[...])
    ckv_ref[...] = ckv
    ckvb_ref[...] = ckv.astype(BF16)
    g0 = Q_LORA + KV_LORA
    sg_ref[...] = _silu(z[:, g0:g0 + MLA_WIDTH])
    t = z[:, g0 + MLA_WIDTH:] * cs_ref[...]
    kpe2 = t + pltpu.roll(t, QK_ROPE, axis=1)
    kpe_ref[...] = kpe2
    kpeb_ref[...] = kpe2.astype(BF16)


def _mla_in(h, g, w, qn, kvn, cs, tm):
    rows = h.shape[0]
    n_cs = cs.shape[0] // tm
    n_in = w.shape[1]
    row = lambda r: (r, 0)
    outs = (
        jax.ShapeDtypeStruct((rows, Q_LORA), BF16),
        jax.ShapeDtypeStruct((rows, KV_LORA), F32),
        jax.ShapeDtypeStruct((rows, KV_LORA), BF16),
        jax.ShapeDtypeStruct((rows, LANES), F32),
        jax.ShapeDtypeStruct((rows, LANES), BF16),
        jax.ShapeDtypeStruct((rows, MLA_WIDTH), F32),
    )
    return pl.pallas_call(
        _mla_in_kernel,
        grid=(rows // tm,),
        in_specs=[
            pl.BlockSpec((tm, D_MODEL), row),
            _full((1, D_MODEL)),
            _full((D_MODEL, n_in)),
            _full((1, Q_LORA)),
            _full((1, KV_LORA)),
            pl.BlockSpec((tm, LANES), lambda r: (r % n_cs, 0)),
        ],
        out_specs=tuple(pl.BlockSpec((tm, o.shape[1]), row) for o in outs),
        out_shape=outs,
        compiler_params=_params("parallel"),
        name="mla_in",
    )(h, g, w, qn, kvn, cs)


def _rope_groups(cq, wr_ref, wrot_ref, cc, ss):
    r = jnp.dot(cq, wr_ref[...], preferred_element_type=F32)
    rr = jnp.dot(cq, wrot_ref[...], preferred_element_type=F32)
    n_grp = r.shape[1] // LANES
    return [r[:, j * LANES:(j + 1) * LANES] * cc + rr[:, j * LANES:(j + 1) * LANES] * ss
            for j in range(n_grp)]


def _mla_qkv_kernel(cq_ref, ckvb_ref, wqn_ref, wqr_ref, wqrot_ref, wuk_ref, wuvt_ref,
                    cc_ref, ss_ref, qn_ref, qr2_ref, kn_ref, vt_ref):
    cq = cq_ref[...]
    qn = jnp.dot(cq, wqn_ref[...], preferred_element_type=F32)
    qn_ref[...] = (qn * MLA_QSCALE).astype(BF16)
    groups = _rope_groups(cq, wqr_ref, wqrot_ref, cc_ref[...], ss_ref[...])
    low = lax.broadcasted_iota(jnp.int32, groups[0].shape, 1) < QK_ROPE
    for j, grp in enumerate(groups):
        grp = grp * MLA_QSCALE
        qr2_ref[:, (2 * j) * LANES:(2 * j + 1) * LANES] = jnp.where(low, grp, 0.0).astype(BF16)
        qr2_ref[:, (2 * j + 1) * LANES:(2 * j + 2) * LANES] = jnp.where(low, 0.0, grp).astype(BF16)
    ckv = ckvb_ref[...]
    kn_ref[...] = jnp.dot(ckv, wuk_ref[...], preferred_element_type=F32).astype(BF16)
    vt_ref[...] = lax.dot_general(wuvt_ref[...], ckv, NT_DIMS,
                                  preferred_element_type=F32).astype(BF16)


def _mla_qkv(cq, ckvb, wqn, wqr, wqrot, wuk, wuvt, cc, ss, tm):
    rows = cq.shape[0]
    n_cs = cc.shape[0] // tm
    row = lambda r: (r, 0)
    tab = pl.BlockSpec((tm, LANES), lambda r: (r % n_cs, 0))
    wide = jax.ShapeDtypeStruct((rows, MLA_WIDTH), BF16)
    return pl.pallas_call(
        _mla_qkv_kernel,
        grid=(rows // tm,),
        in_specs=[
            pl.BlockSpec((tm, Q_LORA), row),
            pl.BlockSpec((tm, KV_LORA), row),
            _full(wqn.shape), _full(wqr.shape), _full(wqrot.shape),
            _full(wuk.shape), _full(wuvt.shape), tab, tab,
        ],
        out_specs=(pl.BlockSpec((tm, MLA_WIDTH), row),) * 3
        + (pl.BlockSpec((MLA_WIDTH, tm), lambda r: (0, r)),),
        out_shape=(wide,) * 3 + (jax.ShapeDtypeStruct((MLA_WIDTH, rows), BF16),),
        compiler_params=_params("parallel"),
        name="mla_qkv",
    )(cq, ckvb, wqn, wqr, wqrot, wuk, wuvt, cc, ss)


def _mla_q_kernel(cq_ref, wqn_ref, wqr_ref, wqrot_ref, cc_ref, ss_ref, qn_ref, qr_ref):
    cq = cq_ref[...]
    qn_ref[...] = jnp.dot(cq, wqn_ref[...], preferred_element_type=F32).astype(BF16)
    groups = _rope_groups(cq, wqr_ref, wqrot_ref, cc_ref[...], ss_ref[...])
    for j, grp in enumerate(groups):
        qr_ref[:, j * LANES:(j + 1) * LANES] = grp.astype(BF16)


def _mla_q(cq, wqn, wqr, wqrot, cc, ss):
    rows = cq.shape[0]
    return pl.pallas_call(
        _mla_q_kernel,
        grid=(1,),
        in_specs=[_full(cq.shape), _full(wqn.shape), _full(wqr.shape), _full(wqrot.shape),
                  _full(cc.shape), _full(ss.shape)],
        out_specs=(_full((rows, MLA_WIDTH)), _full((rows, MLA_HEADS * QK_ROPE))),
        out_shape=(jax.ShapeDtypeStruct((rows, MLA_WIDTH), BF16),
                   jax.ShapeDtypeStruct((rows, MLA_HEADS * QK_ROPE), BF16)),
        compiler_params=_params("arbitrary"),
        name="mla_q_sample",
    )(cq, wqn, wqr, wqrot, cc, ss)


def _mla_flash_kernel(qn_ref, qr_ref, kn_ref, kpe_ref, vt_ref, sg_ref, o_ref, s_scr, acc_scr,
                      *, blk, sub, heads):
    i = pl.program_id(2)
    lanes = [slice(h * LANES, (h + 1) * LANES) for h in range(heads)]
    q = [jnp.concatenate([qn_ref[:, ln], qr_ref[:, ln]], axis=-1) for ln in lanes]

    def scores(h, start, col0, slot):
        rows = pl.ds(start, sub)
        k = jnp.concatenate([kn_ref[rows, lanes[h]], kpe_ref[rows, :]], axis=-1)
        s_scr[h, slot, :, col0:] = lax.dot_general(k, q[h][col0:, :], NT_DIMS,
                                                   preferred_element_type=F32)

    def fold(h, slot, start, col0, carry, diagonal):
        m, l = (c[:, col0:] for c in carry)
        rows = pl.ds(start, sub)
        s = s_scr[h, slot, :, col0:]
        if diagonal:
            ki = lax.broadcasted_iota(jnp.int32, s.shape, 0)
            qi = lax.broadcasted_iota(jnp.int32, s.shape, 1)
            s = jnp.where(ki <= qi, s, NEG_INF)
        m_new = jnp.maximum(m, jnp.max(s, axis=0, keepdims=True))
        alpha = jnp.exp2(m - m_new)
        p = jnp.exp2(s - m_new)
        l = alpha * l + jnp.sum(p, axis=0, keepdims=True)
        acc_scr[h, :, col0:] = alpha * acc_scr[h, :, col0:] + jnp.dot(
            vt_ref[lanes[h], rows], p.astype(BF16), preferred_element_type=F32)
        if col0 == 0:
            return m_new, l
        return tuple(jnp.concatenate([c[:, :col0], n], axis=1)
                     for c, n in zip(carry, (m_new, l)))

    n_sub = blk // sub

    def full_block(j, carries):
        base = pl.multiple_of(j * blk, blk)
        carries = list(carries)
        for u in range(n_sub):
            for h in range(heads):
                scores(h, base + (u + 1) * sub, 0, (u + 1) % 2)
            for h in range(heads):
                carries[h] = fold(h, u % 2, base + u * sub, 0, carries[h], False)
        return tuple(carries)

    init = (jnp.full((1, blk), NEG_INF, F32), jnp.zeros((1, blk), F32))
    acc_scr[...] = jnp.zeros(acc_scr.shape, F32)
    for h in range(heads):
        scores(h, 0, 0, 0)
    carries = list(lax.fori_loop(0, i, full_block, (init,) * heads))
    base = pl.multiple_of(i * blk, blk)
    for u in range(n_sub):
        if u + 1 < n_sub:
            for h in range(heads):
                scores(h, base + (u + 1) * sub, (u + 1) * sub, (u + 1) % 2)
        for h in range(heads):
            carries[h] = fold(h, u % 2, base + u * sub, u * sub, carries[h], True)
    for h in range(heads):
        _, l = carries[h]
        o_ref[:, lanes[h]] = ((acc_scr[h] / l).T * sg_ref[:, lanes[h]]).astype(BF16)


def _mla_flash(qn, qr2, kn, kpeb, vt, sg, blk, sub, heads):
    b, s, _ = qn.shape
    width = heads * LANES
    q_spec = pl.BlockSpec((None, blk, width), lambda bi, h, i: (bi, i, h))
    return pl.pallas_call(
        functools.partial(_mla_flash_kernel, blk=blk, sub=sub, heads=heads),
        grid=(b, MLA_HEADS // heads, s // blk),
        in_specs=[q_spec, q_spec,
                  pl.BlockSpec((None, s, width), lambda bi, h, i: (bi, 0, h)),
                  pl.BlockSpec((None, s, LANES), lambda bi, h, i: (bi, 0, 0)),
                  pl.BlockSpec((heads * V_HEAD, s), lambda bi, h, i: (h, bi)),
                  q_spec],
        out_specs=q_spec,
        out_shape=jax.ShapeDtypeStruct((b, s, MLA_WIDTH), BF16),
        scratch_shapes=[pltpu.VMEM((heads, 2, sub, blk), F32),
                        pltpu.VMEM((heads, V_HEAD, blk), F32)],
        compiler_params=_params("parallel", "parallel", "arbitrary"),
        name="mla_flash",
    )(qn, qr2, kn, kpeb, vt, sg)


def _out_kernel(x_ref, w_ref, g_ref, h_ref, o_ref):
    y = jnp.dot(x_ref[...], w_ref[...], preferred_element_type=F32)
    o_ref[...] = h_ref[...] + _rms(y, g_ref[...])


def _out_proj(x, w, g, h, tm):
    rows, width = x.shape
    row = lambda r: (r, 0)
    return pl.pallas_call(
        _out_kernel,
        grid=(rows // tm,),
        in_specs=[pl.BlockSpec((tm, width), row), _full(w.shape), _full((1, D_MODEL)),
                  pl.BlockSpec((tm, D_MODEL), row)],
        out_specs=pl.BlockSpec((tm, D_MODEL), row),
        out_shape=jax.ShapeDtypeStruct((rows, D_MODEL), F32),
        compiler_params=_params("parallel"),
        name="out_proj",
    )(x, w, g, h)


def _qlat_kernel(qn_ref, wukt_ref, o_ref):
    o_ref[...] = jnp.dot(qn_ref[...], wukt_ref[...], preferred_element_type=F32).astype(BF16)


def _qlat(qn, wukt):
    rows = qn.shape[0]
    return pl.pallas_call(
        _qlat_kernel,
        grid=(MLA_HEADS,),
        in_specs=[pl.BlockSpec((rows, QK_NOPE), lambda h: (0, h)),
                  pl.BlockSpec((None, QK_NOPE, KV_LORA), lambda h: (h, 0, 0))],
        out_specs=pl.BlockSpec((rows, KV_LORA), lambda h: (0, h)),
        out_shape=jax.ShapeDtypeStruct((rows, MLA_HEADS * KV_LORA), BF16),
        compiler_params=_params("parallel"),
        name="mla_qlat",
    )(qn, wukt)


def _paged_kernel(pt_ref, ql_ref, qr_ref, cnew_ref, knew_ref, ckv_hbm, kpe_hbm, o_ref,
                  ckv0, ckv1, kpe0, kpe1, kb, pb, s_scr, sem, *, layer, n_pages, chunk):
    g = pl.program_id(0)
    n_seq = 2 * pl.num_programs(0)
    bufs = ((ckv0, kpe0), (ckv1, kpe1))

    def page_copies(seq, sl, p):
        pid = pt_ref[seq, p]
        ckv_buf, kpe_buf = bufs[sl]
        return (pltpu.make_async_copy(ckv_hbm.at[layer, pid],
                                      ckv_buf.at[pl.ds(p * PAGE_SIZE, PAGE_SIZE), :],
                                      sem.at[0, sl]),
                pltpu.make_async_copy(kpe_hbm.at[layer, pid], kpe_buf.at[p], sem.at[1, sl]))

    def start_page(seq, sl, p):
        for copy in page_copies(seq, sl, p):
            copy.start()

    def wait_gather(seq, sl):
        for p in range(n_pages):
            for copy in page_copies(seq, sl, p):
                copy.wait()

    @pl.when(g == 0)
    def _():
        for p in range(n_pages):
            start_page(0, 0, p)

    for sl in range(2):
        seq = 2 * g + sl
        nxt = jnp.minimum(seq + 1, n_seq - 1)
        ckv_buf, kpe_buf = bufs[sl]
        wait_gather(seq, sl)
        ql = ql_ref[sl]
        qr = qr_ref[sl]
        for c in range(n_pages // chunk):
            for p in range(c * chunk, (c + 1) * chunk):
                start_page(nxt, 1 - sl, p)
                keys = slice(p * PAGE_SIZE, (p + 1) * PAGE_SIZE)
                kb[keys, :] = ckv_buf[keys, :].astype(BF16)
                pb[:, keys] = kpe_buf[p].astype(BF16)
            keys = slice(c * chunk * PAGE_SIZE, (c + 1) * chunk * PAGE_SIZE)
            s_scr[:, keys] = (
                lax.dot_general(ql, kb[keys, :], NT_DIMS, preferred_element_type=F32)
                + jnp.dot(qr, pb[:, keys], preferred_element_type=F32)) * MLA_SCALE
        s = s_scr[...]
        cn = cnew_ref[sl].astype(BF16).astype(F32)
        kn = knew_ref[sl].astype(BF16).astype(F32)
        s_new = (jnp.sum(ql.astype(F32) * cn, axis=-1, keepdims=True)
                 + jnp.sum(qr.astype(F32) * kn, axis=-1, keepdims=True)) * MLA_SCALE
        m = jnp.maximum(jnp.max(s, axis=-1, keepdims=True), s_new)
        e = jnp.exp(s - m)
        e_new = jnp.exp(s_new - m)
        den = jnp.sum(e, axis=-1, keepdims=True) + e_new
        acc = (jnp.dot(e.astype(BF16), kb[...], preferred_element_type=F32)
               + e_new.astype(BF16).astype(F32) * cn)
        o_ref[sl] = acc / den

    @pl.when(g == pl.num_programs(0) - 1)
    def _():
        wait_gather(n_seq - 1, 0)


def _paged_attention(page_table, qlat3, qr3, ckv_new, kpe_new, cache_ckv, cache_kpe_t, layer,
                     chunk):
    nb, n_pages = page_table.shape
    n_keys = n_pages * PAGE_SIZE
    pair = lambda shape: pl.BlockSpec((2,) + shape, lambda g, pt: (g, 0, 0))
    hbm = pl.BlockSpec(memory_space=pl.ANY)
    grid_spec = pltpu.PrefetchScalarGridSpec(
        num_scalar_prefetch=1,
        grid=(nb // 2,),
        in_specs=[pair((MLA_HEADS, KV_LORA)), pair((MLA_HEADS, QK_ROPE)),
                  pair((1, KV_LORA)), pair((1, QK_ROPE)), hbm, hbm],
        out_specs=pair((MLA_HEADS, KV_LORA)),
        scratch_shapes=[
            pltpu.VMEM((n_keys, KV_LORA), F32),
            pltpu.VMEM((n_keys, KV_LORA), F32),
            pltpu.VMEM((n_pages, QK_ROPE, PAGE_SIZE), F32),
            pltpu.VMEM((n_pages, QK_ROPE, PAGE_SIZE), F32),
            pltpu.VMEM((n_keys, KV_LORA), BF16),
            pltpu.VMEM((QK_ROPE, n_keys), BF16),
            pltpu.VMEM((MLA_HEADS, n_keys), F32),
            pltpu.SemaphoreType.DMA((2, 2)),
        ],
    )
    return pl.pallas_call(
        functools.partial(_paged_kernel, layer=layer, n_pages=n_pages, chunk=chunk),
        grid_spec=grid_spec,
        out_shape=jax.ShapeDtypeStruct((nb, MLA_HEADS, KV_LORA), F32),
        compiler_params=_params("arbitrary"),
        name="mla_paged",
    )(page_table, qlat3, qr3, ckv_new, kpe_new, cache_ckv, cache_kpe_t)


def _olat_kernel(ol_ref, wuv_ref, sg_ref, o_ref):
    o = jnp.dot(ol_ref[...].astype(BF16), wuv_ref[...], preferred_element_type=F32)
    o_ref[...] = (o * sg_ref[...]).astype(BF16)


def _olat_proj(olat, wuv3, sg):
    rows = olat.shape[0]
    return pl.pallas_call(
        _olat_kernel,
        grid=(MLA_HEADS,),
        in_specs=[pl.BlockSpec((rows, KV_LORA), lambda h: (0, h)),
                  pl.BlockSpec((None, KV_LORA, V_HEAD), lambda h: (h, 0, 0)),
                  pl.BlockSpec((rows, V_HEAD), lambda h: (0, h))],
        out_specs=pl.BlockSpec((rows, V_HEAD), lambda h: (0, h)),
        out_shape=jax.ShapeDtypeStruct((rows, MLA_WIDTH), BF16),
        compiler_params=_params("parallel"),
        name="mla_olat",
    )(olat, wuv3, sg)


def _swa_in_kernel(h_ref, g_ref, w_ref, q_ref, k_ref, v_ref, sg_ref):
    a = _rms(h_ref[...], g_ref[...]).astype(BF16)
    z = jnp.dot(a, w_ref[...], preferred_element_type=F32)
    q_ref[...] = (z[:, :SWA_Q] * SWA_SCALE).astype(BF16)
    k_ref[...] = z[:, SWA_Q:SWA_Q + SWA_KV]
    v_ref[...] = z[:, SWA_Q + SWA_KV:SWA_Q + 2 * SWA_KV]
    sg_ref[...] = _silu(z[:, SWA_Q + 2 * SWA_KV:])


def _swa_in(h, g, w, tm):
    rows = h.shape[0]
    row = lambda r: (r, 0)
    outs = (jax.ShapeDtypeStruct((rows, SWA_Q), BF16),
            jax.ShapeDtypeStruct((rows, SWA_KV), F32),
            jax.ShapeDtypeStruct((rows, SWA_KV), F32),
            jax.ShapeDtypeStruct((rows, SWA_Q), F32))
    return pl.pallas_call(
        _swa_in_kernel,
        grid=(rows // tm,),
        in_specs=[pl.BlockSpec((tm, D_MODEL), row), _full((1, D_MODEL)), _full(w.shape)],
        out_specs=tuple(pl.BlockSpec((tm, o.shape[1]), row) for o in outs),
        out_shape=outs,
        compiler_params=_params("parallel"),
        name="swa_in",
    )(h, g, w)


def _sink_softmax(s, sink):
    m = jnp.maximum(jnp.max(s, axis=-1, keepdims=True), sink)
    e = jnp.exp(s - m)
    return e / (jnp.sum(e, axis=-1, keepdims=True) + jnp.exp(sink - m))


def _alibi_slope(head):
    return 2.0 ** (-8.0 * (head + 1) / SWA_HEADS)


def _swa_bias():
    dist = (WINDOW + np.arange(WINDOW))[None, :] - np.arange(2 * WINDOW)[:, None]
    in_window = (dist >= 0) & (dist < WINDOW)
    slopes = np.asarray([_alibi_slope(h) for h in range(SWA_HEADS)], np.float32)
    bias = -slopes[:, None, None] * dist[None].astype(np.float32)
    return np.where(in_window[None], bias, np.float32(NEG_INF)).astype(np.float32)


def _swa_prompt_kernel(sink_ref, q_ref, kp_ref, kc_ref, vp_ref, vc_ref, sg_ref, bias_ref, o_ref):
    n = pl.program_id(1)
    kk = jnp.concatenate([kp_ref[...], kc_ref[...]], axis=0).astype(BF16)
    vvt = jnp.concatenate([vp_ref[...], vc_ref[...]], axis=0).T.astype(BF16)
    k_low = lax.broadcasted_iota(jnp.int32, (2 * WINDOW, LANES), 1) < SWA_HEAD_DIM
    v_low = lax.broadcasted_iota(jnp.int32, (LANES, 2 * WINDOW), 0) < SWA_HEAD_DIM
    zero = jnp.zeros((), BF16)
    prev_pen = jnp.where(n > 0, 0.0, NEG_INF).astype(F32)
    span = lambda g: slice(g * WINDOW, (g + 1) * WINDOW)
    pairs = range(SWA_KV_HEADS // 2)
    kv_span = [slice(pair * LANES, (pair + 1) * LANES) for pair in pairs]
    grp_lanes = [[slice((pair * SWA_GROUP + g) * LANES, (pair * SWA_GROUP + g + 1) * LANES)
                  for g in range(SWA_GROUP)] for pair in pairs]
    st_all = {}
    for pair in pairs:
        q_st = jnp.concatenate([q_ref[:, lanes] for lanes in grp_lanes[pair]], axis=0)
        for par in range(2):
            keep_k = k_low if par == 0 else ~k_low
            st_all[pair, par] = lax.dot_general(
                jnp.where(keep_k, kk[:, kv_span[pair]], zero), q_st, NT_DIMS,
                preferred_element_type=F32)
    for pair in pairs:
        vt_g = vvt[kv_span[pair], :]
        ot_pair = jnp.zeros((LANES, SWA_GROUP * WINDOW), F32)
        for par in range(2):
            keep_v = v_low if par == 0 else ~v_low
            es, inv_dens = [], []
            for g in range(SWA_GROUP):
                head = SWA_HEAD_ORDER[2 * (pair * SWA_GROUP + g) + par]
                s = st_all[pair, par][:, span(g)] + bias_ref[head]
                s_prev = s[:WINDOW, :] + prev_pen
                s_cur = s[WINDOW:, :]
                sink = sink_ref[head]
                m = jnp.maximum(jnp.maximum(jnp.max(s_prev, axis=0, keepdims=True),
                                            jnp.max(s_cur, axis=0, keepdims=True)), sink)
                e_prev = jnp.exp(s_prev - m)
                e_cur = jnp.exp(s_cur - m)
                den = (jnp.sum(e_prev, axis=0, keepdims=True)
                       + jnp.sum(e_cur, axis=0, keepdims=True) + jnp.exp(sink - m))
                es.append(jnp.concatenate([e_prev, e_cur], axis=0).astype(BF16))
                inv_dens.append(1.0 / den)
            ot_par = jnp.dot(jnp.where(keep_v, vt_g, zero), jnp.concatenate(es, axis=1),
                             preferred_element_type=F32)
            ot_pair = ot_pair + ot_par * jnp.concatenate(inv_dens, axis=1)
        o_pair = ot_pair.T
        for g, lanes in enumerate(grp_lanes[pair]):
            o_ref[:, lanes] = (o_pair[span(g), :] * sg_ref[:, lanes]).astype(BF16)


def _swa_prompt(sinks, q, k, v, sg):
    b, s, _ = q.shape
    cur = lambda bi, n: (bi, n, 0)
    prev = lambda bi, n: (bi, jnp.maximum(n - 1, 0), 0)
    wide = lambda idx: pl.BlockSpec((None, WINDOW, SWA_Q), idx)
    narrow = lambda idx: pl.BlockSpec((None, WINDOW, SWA_KV), idx)
    bias = _swa_bias()
    return pl.pallas_call(
        _swa_prompt_kernel,
        grid=(b, s // WINDOW),
        in_specs=[pl.BlockSpec(memory_space=pltpu.SMEM),
                  wide(cur), narrow(prev), narrow(cur), narrow(prev), narrow(cur), wide(cur),
                  _full(bias.shape)],
        out_specs=wide(cur),
        out_shape=jax.ShapeDtypeStruct((b, s, SWA_Q), BF16),
        compiler_params=_params("parallel", "parallel"),
        name="swa_prompt",
    )(sinks, q, k, k, v, v, sg, jnp.asarray(bias))


def _swa_sample_kernel(q_ref, bk_ref, bv_ref, kn_ref, vn_ref, sg_ref, sink_ref, slope_ref,
                       mask_ref, nk_ref, nv_ref, o_ref):
    n_seq = q_ref.shape[0]
    mask = mask_ref[...] > 0.5
    scores = []
    for j in range(n_seq):
        nk_ref[j, 0:WINDOW - 1, :] = bk_ref[j, 1:WINDOW, :]
        nk_ref[j, WINDOW - 1:WINDOW, :] = kn_ref[j]
        nv_ref[j, 0:WINDOW - 1, :] = bv_ref[j, 1:WINDOW, :]
        nv_ref[j, WINDOW - 1:WINDOW, :] = vn_ref[j]
        q_exp = jnp.where(mask, jnp.concatenate([q_ref[j]] * SWA_KV_HEADS, axis=-1),
                          jnp.zeros((), BF16))
        scores.append(lax.dot_general(q_exp, nk_ref[j].astype(BF16), NT_DIMS,
                                      preferred_element_type=F32))
    dist = (WINDOW - 1 - lax.broadcasted_iota(jnp.int32, scores[0].shape, 1)).astype(F32)
    bias = slope_ref[...] * dist
    probs = [_sink_softmax(s - bias, sink_ref[...]).astype(BF16) for s in scores]
    for j in range(n_seq):
        o_all = jnp.where(mask, jnp.dot(probs[j], nv_ref[j].astype(BF16),
                                        preferred_element_type=F32), 0.0)
        o = o_all[:, 0:SWA_HEAD_DIM]
        for kh in range(1, SWA_KV_HEADS):
            o = o + o_all[:, kh * SWA_HEAD_DIM:(kh + 1) * SWA_HEAD_DIM]
        o_ref[j] = (o * sg_ref[j]).astype(BF16)


def _swa_sample(q3, buf_k, buf_v, k_new, v_new, sg3, sink_col, slope_col, mask, layer, per_step):
    nb = q3.shape[0]
    per_b = lambda shape: pl.BlockSpec((per_step,) + shape, lambda b: (b, 0, 0))
    buf = pl.BlockSpec((None, per_step, WINDOW, SWA_KV), lambda b: (layer, b, 0, 0))
    return pl.pallas_call(
        _swa_sample_kernel,
        grid=(nb // per_step,),
        in_specs=[per_b((SWA_HEADS, SWA_HEAD_DIM)), buf, buf,
                  per_b((1, SWA_KV)), per_b((1, SWA_KV)), per_b((SWA_HEADS, SWA_HEAD_DIM)),
                  _full((SWA_HEADS, 1)), _full((SWA_HEADS, 1)), _full((SWA_HEADS, SWA_KV))],
        out_specs=(per_b((WINDOW, SWA_KV)), per_b((WINDOW, SWA_KV)),
                   per_b((SWA_HEADS, SWA_HEAD_DIM))),
        out_shape=(jax.ShapeDtypeStruct((nb, WINDOW, SWA_KV), F32),
                   jax.ShapeDtypeStruct((nb, WINDOW, SWA_KV), F32),
                   jax.ShapeDtypeStruct((nb, SWA_HEADS, SWA_HEAD_DIM), BF16)),
        compiler_params=_params("parallel"),
        name="swa_sample",
    )(q3, buf_k, buf_v, k_new, v_new, sg3, sink_col, slope_col, mask)


def _rot_cols(w):
    half = QK_ROPE // 2
    return jnp.concatenate([-w[..., half:], w[..., :half]], axis=-1)


def _rope_tables(pos):
    inv = ROPE_THETA ** (-jnp.arange(0, QK_ROPE, 2, dtype=F32) / QK_ROPE)
    ang = pos.astype(F32)[:, None] * inv[None, :]
    ang = jnp.concatenate([ang, ang], axis=-1)
    cos, sin = jnp.cos(ang), jnp.sin(ang)
    return (jnp.concatenate([cos, sin], axis=-1), jnp.concatenate([cos, cos], axis=-1),
            jnp.concatenate([sin, sin], axis=-1))


def _mla_weights(w_in, w_qb, w_uk, w_uv, w_o):
    g0 = Q_LORA + KV_LORA
    w_kpe = w_in[:, g0:g0 + QK_ROPE]
    w_in_r = jnp.concatenate(
        [w_in[:, :g0], w_in[:, g0 + QK_ROPE:], w_kpe, _rot_cols(w_kpe)], axis=1).astype(BF16)
    wqn = w_qb[:, :, :QK_NOPE].reshape(Q_LORA, MLA_HEADS * QK_NOPE).astype(BF16)
    wqr3 = w_qb[:, :, QK_NOPE:]
    wqr = wqr3.reshape(Q_LORA, MLA_HEADS * QK_ROPE).astype(BF16)
    wqrot = _rot_cols(wqr3).reshape(Q_LORA, MLA_HEADS * QK_ROPE).astype(BF16)
    wuk = w_uk.reshape(KV_LORA, MLA_HEADS * QK_NOPE).astype(BF16)
    wuvt = w_uv.reshape(KV_LORA, MLA_HEADS * V_HEAD).T.astype(BF16)
    wukt = jnp.transpose(w_uk, (1, 2, 0)).astype(BF16)
    wuv3 = jnp.transpose(w_uv, (1, 0, 2)).astype(BF16)
    return w_in_r, wqn, wqr, wqrot, wuk, wuvt, wukt, wuv3, w_o.astype(BF16)


def _swa_weights(w_in, w_o):
    cols = np.concatenate([np.arange(h * SWA_HEAD_DIM, (h + 1) * SWA_HEAD_DIM)
                           for h in SWA_HEAD_ORDER])
    g0 = SWA_Q + 2 * SWA_KV
    w_in_r = jnp.concatenate([w_in[:, :SWA_Q][:, cols], w_in[:, SWA_Q:g0], w_in[:, g0:][:, cols]],
                             axis=1).astype(BF16)
    return w_in_r, w_o[cols, :].astype(BF16)


def kernel(x_prompt, x_sample, cache_ckv, cache_kpe, state_swa_k, state_swa_v, page_table,
           pre_norm, post_norm, mla_w_in, mla_q_norm, mla_w_qb, mla_kv_norm, mla_w_uk,
           mla_w_uv, mla_w_o, swa_w_in, swa_sinks, swa_w_o):
    n_b, seq, _ = x_prompt.shape
    n_db, t_new, _ = x_sample.shape
    assert t_new == 1 and seq % 512 == 0 and state_swa_k.shape[2] == WINDOW
    depth = pre_norm.shape[0]
    past_len = page_table.shape[1] * PAGE_SIZE

    cs_p, cc_p, ss_p = _rope_tables(jnp.arange(seq, dtype=jnp.int32))
    cs_s, cc_s, ss_s = (jnp.broadcast_to(t, (n_db, LANES))
                        for t in _rope_tables(jnp.full((1,), past_len, jnp.int32)))

    order = np.asarray(SWA_HEAD_ORDER)
    slope_col = jnp.asarray([[_alibi_slope(h)] for h in SWA_HEAD_ORDER], F32)
    lane_kv = np.arange(SWA_KV)[None, :] // SWA_HEAD_DIM
    mask = jnp.asarray(lane_kv == (order // SWA_GROUP)[:, None], F32)
    cache_kpe_t = jnp.swapaxes(cache_kpe, 2, 3)
    buf_k = state_swa_k.reshape(state_swa_k.shape[:3] + (SWA_KV,))
    buf_v = state_swa_v.reshape(state_swa_v.shape[:3] + (SWA_KV,))

    hp = x_prompt.reshape(n_b * seq, D_MODEL)
    hs = x_sample.reshape(n_db, D_MODEL)
    ckv_p, kpe_p, ckv_s, kpe_s = [], [], [], []
    swk_p, swv_p, swk_s, swv_s = [], [], [], []
    for i in range(depth):
        l = i // 2
        g_pre = pre_norm[i][None, :]
        g_post = post_norm[i][None, :]
        if i % 2 == 0:
            w_in, wqn, wqr, wqrot, wuk, wuvt, wukt, wuv3, w_o = _mla_weights(
                mla_w_in[l], mla_w_qb[l], mla_w_uk[l], mla_w_uv[l], mla_w_o[l])
            qn_g = mla_q_norm[l][None, :]
            kvn_g = mla_kv_norm[l][None, :]
            cq, ckv, ckvb, kpe2, kpeb, sg = _mla_in(hp, g_pre, w_in, qn_g, kvn_g, cs_p, 512)
            qn, qr2, kn, vt = _mla_qkv(cq, ckvb, wqn, wqr, wqrot, wuk, wuvt, cc_p, ss_p, 256)
            to3 = lambda a: a.reshape(n_b, seq, a.shape[-1])
            og = _mla_flash(to3(qn), to3(qr2), to3(kn), to3(kpeb), vt, to3(sg), 512, 256, 4)
            hp = _out_proj(og.reshape(n_b * seq, MLA_WIDTH), w_o, g_post, hp, 512)
            ckv_p.append(ckv.reshape(n_b, seq, KV_LORA))
            kpe_p.append(kpe2[:, :QK_ROPE].reshape(n_b, seq, QK_ROPE))
            cq, ckv, _, kpe2, _, sg = _mla_in(hs, g_pre, w_in, qn_g, kvn_g, cs_s, n_db)
            qn, qr = _mla_q(cq, wqn, wqr, wqrot, cc_s, ss_s)
            qlat = _qlat(qn, wukt)
            kpe_new = kpe2[:, :QK_ROPE]
            olat = _paged_attention(
                page_table, qlat.reshape(n_db, MLA_HEADS, KV_LORA),
                qr.reshape(n_db, MLA_HEADS, QK_ROPE), ckv.reshape(n_db, 1, KV_LORA),
                kpe_new.reshape(n_db, 1, QK_ROPE), cache_ckv, cache_kpe_t, l, 16)
            og = _olat_proj(olat.reshape(n_db, MLA_HEADS * KV_LORA), wuv3, sg)
            hs = _out_proj(og, w_o, g_post, hs, n_db)
            ckv_s.append(ckv.reshape(n_db, 1, KV_LORA))
            kpe_s.append(kpe_new.reshape(n_db, 1, QK_ROPE))
        else:
            w_in, w_o = _swa_weights(swa_w_in[l], swa_w_o[l])
            sinks = swa_sinks[l]
            q, k, v, sg = _swa_in(hp, g_pre, w_in, 512)
            to3 = lambda a: a.reshape(n_b, seq, a.shape[-1])
            og = _swa_prompt(sinks, to3(q), to3(k), to3(v), to3(sg))
            hp = _out_proj(og.reshape(n_b * seq, SWA_Q), w_o, g_post, hp, 512)
            keep = min(WINDOW, seq)
            tail = lambda a: to3(a)[:, seq - keep:].reshape(n_b, keep, SWA_KV_HEADS, SWA_HEAD_DIM)
            swk_p.append(tail(k))
            swv_p.append(tail(v))
            q, k, v, sg = _swa_in(hs, g_pre, w_in, n_db)
            nk, nv, og3 = _swa_sample(
                q.reshape(n_db, SWA_HEADS, SWA_HEAD_DIM), buf_k, buf_v,
                k.reshape(n_db, 1, SWA_KV), v.reshape(n_db, 1, SWA_KV),
                sg.reshape(n_db, SWA_HEADS, SWA_HEAD_DIM),
                sinks[order][:, None], slope_col, mask, l, 8)
            hs = _out_proj(og3.reshape(n_db, SWA_Q), w_o, g_post, hs, n_db)
            swk_s.append(nk.reshape(n_db, WINDOW, SWA_KV_HEADS, SWA_HEAD_DIM))
            swv_s.append(nv.reshape(n_db, WINDOW, SWA_KV_HEADS, SWA_HEAD_DIM))
    return (hp.reshape(n_b, seq, D_MODEL), hs.reshape(n_db, t_new, D_MODEL),
            jnp.stack(ckv_p), jnp.stack(kpe_p), jnp.stack(ckv_s), jnp.stack(kpe_s),
            jnp.stack(swk_p), jnp.stack(swv_p), jnp.stack(swk_s), jnp.stack(swv_s))
```

```python
import functools
import math

import jax
import jax.numpy as jnp
import numpy as np
from jax import lax
from jax.experimental import pallas as pl
from jax.experimental.pallas import tpu as pltpu

F32 = jnp.float32
BF16 = jnp.bfloat16

D_MODEL = 1024
NORM_EPS = 1e-6
NEG_INF = -1e30
PAGE_SIZE = 128

MLA_HEADS = 16
Q_LORA = 512
KV_LORA = 256
QK_NOPE = 128
QK_ROPE = 64
V_HEAD = 128
ROPE_THETA = 10000.0
MLA_WIDTH = MLA_HEADS * V_HEAD
MLA_SCALE = 1.0 / math.sqrt(QK_NOPE + QK_ROPE)
MLA_QSCALE = MLA_SCALE * math.log2(math.e)

SWA_HEADS = 16
SWA_KV_HEADS = 4
SWA_GROUP = SWA_HEADS // SWA_KV_HEADS
SWA_HEAD_DIM = 64
WINDOW = 128
SWA_Q = SWA_HEADS * SWA_HEAD_DIM
SWA_KV = SWA_KV_HEADS * SWA_HEAD_DIM
SWA_SCALE = 1.0 / math.sqrt(SWA_HEAD_DIM)

LANES = 128
VMEM_LIMIT = 56 * 1024 * 1024

SWA_HEAD_ORDER = tuple(
    (2 * i + par) * SWA_GROUP + g
    for i in range(SWA_KV_HEADS // 2) for g in range(SWA_GROUP) for par in range(2))

NT_DIMS = (((1,), (1,)), ((), ()))


def _params(*sem):
    return pltpu.CompilerParams(dimension_semantics=sem, vmem_limit_bytes=VMEM_LIMIT)


def _rms(x, g):
    return x * lax.rsqrt(jnp.mean(x * x, axis=-1, keepdims=True) + NORM_EPS) * g


def _silu(x):
    return x * (1.0 / (1.0 + jnp.exp(-x)))


def _full(shape):
    return pl.BlockSpec(shape, lambda *_: (0,) * len(shape))


def _mla_in_kernel(h_ref, g_ref, w_ref, qn_ref, kvn_ref, cs_ref,
                   cq_ref, ckv_ref, ckvb_ref, kpe_ref, kpeb_ref, sg_ref):
    a = _rms(h_ref[...], g_ref[...]).astype(BF16)
    z = jnp.dot(a, w_ref[...], preferred_element_type=F32)
    cq_ref[...] = _rms(z[:, :Q_LORA], qn_ref[...]).astype(BF16)
    ckv = _rms(z[:, Q_LORA:Q_LORA + KV_LORA], kvn_ref[...])
    ckv_ref[...] = ckv
    ckvb_ref[...] = ckv.astype(BF16)
    g0 = Q_LORA + KV_LORA
    sg_ref[...] = _silu(z[:, g0:g0 + MLA_WIDTH])
    t = z[:, g0 + MLA_WIDTH:] * cs_ref[...]
    kpe2 = t + pltpu.roll(t, QK_ROPE, axis=1)
    kpe_ref[...] = kpe2
    kpeb_ref[...] = kpe2.astype(BF16)


def _mla_in(h, g, w, qn, kvn, cs, tm):
    rows = h.shape[0]
    n_cs = cs.shape[0] // tm
    n_in = w.shape[1]
    row = lambda r: (r, 0)
    outs = (
        jax.ShapeDtypeStruct((rows, Q_LORA), BF16),
        jax.ShapeDtypeStruct((rows, KV_LORA), F32),
        jax.ShapeDtypeStruct((rows, KV_LORA), BF16),
        jax.ShapeDtypeStruct((rows, LANES), F32),
        jax.ShapeDtypeStruct((rows, LANES), BF16),
        jax.ShapeDtypeStruct((rows, MLA_WIDTH), F32),
    )
    return pl.pallas_call(
        _mla_in_kernel,
        grid=(rows // tm,),
        in_specs=[
            pl.BlockSpec((tm, D_MODEL), row),
            _full((1, D_MODEL)),
            _full((D_MODEL, n_in)),
            _full((1, Q_LORA)),
            _full((1, KV_LORA)),
            pl.BlockSpec((tm, LANES), lambda r: (r % n_cs, 0)),
        ],
        out_specs=tuple(pl.BlockSpec((tm, o.shape[1]), row) for o in outs),
        out_shape=outs,
        compiler_params=_params("parallel"),
        name="mla_in",
    )(h, g, w, qn, kvn, cs)


def _rope_groups(cq, wr_ref, wrot_ref, cc, ss):
    r = jnp.dot(cq, wr_ref[...], preferred_element_type=F32)
    rr = jnp.dot(cq, wrot_ref[...], preferred_element_type=F32)
    n_grp = r.shape[1] // LANES
    return [r[:, j * LANES:(j + 1) * LANES] * cc + rr[:, j * LANES:(j + 1) * LANES] * ss
            for j in range(n_grp)]


def _mla_qkv_kernel(cq_ref, ckvb_ref, wqn_ref, wqr_ref, wqrot_ref, wuk_ref, wuvt_ref,
                    cc_ref, ss_ref, qn_ref, qr2_ref, kn_ref, vt_ref):
    cq = cq_ref[...]
    qn = jnp.dot(cq, wqn_ref[...], preferred_element_type=F32)
    qn_ref[...] = (qn * MLA_QSCALE).astype(BF16)
    groups = _rope_groups(cq, wqr_ref, wqrot_ref, cc_ref[...], ss_ref[...])
    low = lax.broadcasted_iota(jnp.int32, groups[0].shape, 1) < QK_ROPE
    for j, grp in enumerate(groups):
        grp = grp * MLA_QSCALE
        qr2_ref[:, (2 * j) * LANES:(2 * j + 1) * LANES] = jnp.where(low, grp, 0.0).astype(BF16)
        qr2_ref[:, (2 * j + 1) * LANES:(2 * j + 2) * LANES] = jnp.where(low, 0.0, grp).astype(BF16)
    ckv = ckvb_ref[...]
    kn_ref[...] = jnp.dot(ckv, wuk_ref[...], preferred_element_type=F32).astype(BF16)
    vt_ref[...] = lax.dot_general(wuvt_ref[...], ckv, NT_DIMS,
                                  preferred_element_type=F32).astype(BF16)


def _mla_qkv(cq, ckvb, wqn, wqr, wqrot, wuk, wuvt, cc, ss, tm):
    rows = cq.shape[0]
    n_cs = cc.shape[0] // tm
    row = lambda r: (r, 0)
    tab = pl.BlockSpec((tm, LANES), lambda r: (r % n_cs, 0))
    wide = jax.ShapeDtypeStruct((rows, MLA_WIDTH), BF16)
    return pl.pallas_call(
        _mla_qkv_kernel,
        grid=(rows // tm,),
        in_specs=[
            pl.BlockSpec((tm, Q_LORA), row),
            pl.BlockSpec((tm, KV_LORA), row),
            _full(wqn.shape), _full(wqr.shape), _full(wqrot.shape),
            _full(wuk.shape), _full(wuvt.shape), tab, tab,
        ],
        out_specs=(pl.BlockSpec((tm, MLA_WIDTH), row),) * 3
        + (pl.BlockSpec((MLA_WIDTH, tm), lambda r: (0, r)),),
        out_shape=(wide,) * 3 + (jax.ShapeDtypeStruct((MLA_WIDTH, rows), BF16),),
        compiler_params=_params("parallel"),
        name="mla_qkv",
    )(cq, ckvb, wqn, wqr, wqrot, wuk, wuvt, cc, ss)


def _mla_q_kernel(cq_ref, wqn_ref, wqr_ref, wqrot_ref, cc_ref, ss_ref, qn_ref, qr_ref):
    cq = cq_ref[...]
    qn_ref[...] = jnp.dot(cq, wqn_ref[...], preferred_element_type=F32).astype(BF16)
    groups = _rope_groups(cq, wqr_ref, wqrot_ref, cc_ref[...], ss_ref[...])
    for j, grp in enumerate(groups):
        qr_ref[:, j * LANES:(j + 1) * LANES] = grp.astype(BF16)


def _mla_q(cq, wqn, wqr, wqrot, cc, ss):
    rows = cq.shape[0]
    return pl.pallas_call(
        _mla_q_kernel,
        grid=(1,),
        in_specs=[_full(cq.shape), _full(wqn.shape), _full(wqr.shape), _full(wqrot.shape),
                  _full(cc.shape), _full(ss.shape)],
        out_specs=(_full((rows, MLA_WIDTH)), _full((rows, MLA_HEADS * QK_ROPE))),
        out_shape=(jax.ShapeDtypeStruct((rows, MLA_WIDTH), BF16),
                   jax.ShapeDtypeStruct((rows, MLA_HEADS * QK_ROPE), BF16)),
        compiler_params=_params("arbitrary"),
        name="mla_q_sample",
    )(cq, wqn, wqr, wqrot, cc, ss)


def _mla_flash_kernel(qn_ref, qr_ref, kn_ref, kpe_ref, vt_ref, sg_ref, o_ref, s_scr, mx_scr,
                      acc_scr, *, blk, sub, heads):
    i = pl.program_id(2)
    lanes = [slice(h * LANES, (h + 1) * LANES) for h in range(heads)]
    q = [jnp.concatenate([qn_ref[:, ln], qr_ref[:, ln]], axis=-1) for ln in lanes]

    def scores(h, start, col0, slot, diagonal):
        rows = pl.ds(start, sub)
        k = jnp.concatenate([kn_ref[rows, lanes[h]], kpe_ref[rows, :]], axis=-1)
        s = lax.dot_general(k, q[h][col0:, :], NT_DIMS, preferred_element_type=F32)
        if diagonal:
            s = causal(s)
        s_scr[h, slot, :, col0:] = s
        mx_scr[h, slot, :, col0:] = jnp.max(s, axis=0, keepdims=True)

    def causal(s):
        ki = lax.broadcasted_iota(jnp.int32, s.shape, 0)
        qi = lax.broadcasted_iota(jnp.int32, s.shape, 1)
        return jnp.where(ki <= qi, s, NEG_INF)

    def fold(h, slot, start, col0, carry, mask_now):
        m, l = (c[:, col0:] for c in carry)
        rows = pl.ds(start, sub)
        s = s_scr[h, slot, :, col0:]
        if mask_now:
            s = causal(s)
            m_new = jnp.maximum(m, jnp.max(s, axis=0, keepdims=True))
        else:
            m_new = jnp.maximum(m, mx_scr[h, slot, :, col0:])
        alpha = jnp.exp2(m - m_new)
        p = jnp.exp2(s - m_new)
        l = alpha * l + jnp.sum(p, axis=0, keepdims=True)
        acc_scr[h, :, col0:] = alpha * acc_scr[h, :, col0:] + jnp.dot(
            vt_ref[lanes[h], rows], p.astype(BF16), preferred_element_type=F32)
        if col0 == 0:
            return m_new, l
        return tuple(jnp.concatenate([c[:, :col0], n], axis=1)
                     for c, n in zip(carry, (m_new, l)))

    n_sub = blk // sub

    def full_block(j, carries):
        base = pl.multiple_of(j * blk, blk)
        carries = list(carries)
        for u in range(n_sub):
            for h in range(heads):
                scores(h, base + (u + 1) * sub, 0, (u + 1) % 2, False)
            for h in range(heads):
                carries[h] = fold(h, u % 2, base + u * sub, 0, carries[h], False)
        return tuple(carries)

    init = (jnp.full((1, blk), NEG_INF, F32), jnp.zeros((1, blk), F32))
    acc_scr[...] = jnp.zeros(acc_scr.shape, F32)
    for h in range(heads):
        scores(h, 0, 0, 0, False)
    carries = list(lax.fori_loop(0, i, full_block, (init,) * heads))
    base = pl.multiple_of(i * blk, blk)
    for u in range(n_sub):
        if u + 1 < n_sub:
            for h in range(heads):
                scores(h, base + (u + 1) * sub, (u + 1) * sub, (u + 1) % 2, True)
        for h in range(heads):
            carries[h] = fold(h, u % 2, base + u * sub, u * sub, carries[h], u == 0)
    for h in range(heads):
        _, l = carries[h]
        o_ref[:, lanes[h]] = ((acc_scr[h] / l).T * sg_ref[:, lanes[h]]).astype(BF16)


def _mla_flash(qn, qr2, kn, kpeb, vt, sg, blk, sub, heads):
    b, s, _ = qn.shape
    width = heads * LANES
    q_spec = pl.BlockSpec((None, blk, width), lambda bi, h, i: (bi, i, h))
    return pl.pallas_call(
        functools.partial(_mla_flash_kernel, blk=blk, sub=sub, heads=heads),
        grid=(b, MLA_HEADS // heads, s // blk),
        in_specs=[q_spec, q_spec,
                  pl.BlockSpec((None, s, width), lambda bi, h, i: (bi, 0, h)),
                  pl.BlockSpec((None, s, LANES), lambda bi, h, i: (bi, 0, 0)),
                  pl.BlockSpec((heads * V_HEAD, s), lambda bi, h, i: (h, bi)),
                  q_spec],
        out_specs=q_spec,
        out_shape=jax.ShapeDtypeStruct((b, s, MLA_WIDTH), BF16),
        scratch_shapes=[pltpu.VMEM((heads, 2, sub, blk), F32),
                        pltpu.VMEM((heads, 2, 1, blk), F32),
                        pltpu.VMEM((heads, V_HEAD, blk), F32)],
        compiler_params=_params("parallel", "parallel", "arbitrary"),
        name="mla_flash",
    )(qn, qr2, kn, kpeb, vt, sg)


def _out_kernel(x_ref, w_ref, g_ref, h_ref, o_ref):
    y = jnp.dot(x_ref[...], w_ref[...], preferred_element_type=F32)
    o_ref[...] = h_ref[...] + _rms(y, g_ref[...])


def _out_proj(x, w, g, h, tm):
    rows, width = x.shape
    row = lambda r: (r, 0)
    return pl.pallas_call(
        _out_kernel,
        grid=(rows // tm,),
        in_specs=[pl.BlockSpec((tm, width), row), _full(w.shape), _full((1, D_MODEL)),
                  pl.BlockSpec((tm, D_MODEL), row)],
        out_specs=pl.BlockSpec((tm, D_MODEL), row),
        out_shape=jax.ShapeDtypeStruct((rows, D_MODEL), F32),
        compiler_params=_params("parallel"),
        name="out_proj",
    )(x, w, g, h)


def _qlat_kernel(qn_ref, wukt_ref, o_ref):
    o_ref[...] = jnp.dot(qn_ref[...], wukt_ref[...], preferred_element_type=F32).astype(BF16)


def _qlat(qn, wukt):
    rows = qn.shape[0]
    return pl.pallas_call(
        _qlat_kernel,
        grid=(MLA_HEADS,),
        in_specs=[pl.BlockSpec((rows, QK_NOPE), lambda h: (0, h)),
                  pl.BlockSpec((None, QK_NOPE, KV_LORA), lambda h: (h, 0, 0))],
        out_specs=pl.BlockSpec((rows, KV_LORA), lambda h: (0, h)),
        out_shape=jax.ShapeDtypeStruct((rows, MLA_HEADS * KV_LORA), BF16),
        compiler_params=_params("parallel"),
        name="mla_qlat",
    )(qn, wukt)


def _paged_kernel(pt_ref, ql_ref, qr_ref, cnew_ref, knew_ref, ckv_hbm, kpe_hbm, o_ref,
                  ckv0, ckv1, kpe0, kpe1, kb, pb, s_scr, e_scr, sem, *, layer, n_pages, chunk):
    g = pl.program_id(0)
    n_seq = 2 * pl.num_programs(0)
    bufs = ((ckv0, kpe0), (ckv1, kpe1))

    def page_copies(seq, sl, p):
        pid = pt_ref[seq, p]
        ckv_buf, kpe_buf = bufs[sl]
        return (pltpu.make_async_copy(ckv_hbm.at[layer, pid],
                                      ckv_buf.at[pl.ds(p * PAGE_SIZE, PAGE_SIZE), :],
                                      sem.at[0, sl]),
                pltpu.make_async_copy(kpe_hbm.at[layer, pid], kpe_buf.at[p], sem.at[1, sl]))

    def start_page(seq, sl, p):
        for i, copy in enumerate(page_copies(seq, sl, p)):
            copy.start(priority=(p + i) % 2)

    def wait_gather(seq, sl):
        for p in range(n_pages):
            for copy in page_copies(seq, sl, p):
                copy.wait()

    def chunk_keys(c):
        return slice(c * chunk * PAGE_SIZE, (c + 1) * chunk * PAGE_SIZE)

    def score_chunk(sl, c, nxt):
        ckv_buf, kpe_buf = bufs[sl]
        for p in range(c * chunk, (c + 1) * chunk):
            start_page(nxt, 1 - sl, p)
            keys = slice(p * PAGE_SIZE, (p + 1) * PAGE_SIZE)
            kb[sl, keys, :] = ckv_buf[keys, :].astype(BF16)
            pb[:, keys] = kpe_buf[p].astype(BF16)
        keys = chunk_keys(c)
        s_scr[:, keys] = (
            lax.dot_general(ql_ref[sl], kb[sl, keys, :], NT_DIMS, preferred_element_type=F32)
            + jnp.dot(qr_ref[sl], pb[:, keys], preferred_element_type=F32)) * MLA_SCALE

    def softmax(sl):
        s = s_scr[...]
        cn = cnew_ref[sl].astype(BF16).astype(F32)
        kn = knew_ref[sl].astype(BF16).astype(F32)
        s_new = (jnp.sum(ql_ref[sl].astype(F32) * cn, axis=-1, keepdims=True)
                 + jnp.sum(qr_ref[sl].astype(F32) * kn, axis=-1, keepdims=True)) * MLA_SCALE
        m = jnp.maximum(jnp.max(s, axis=-1, keepdims=True), s_new)
        e = jnp.exp(s - m)
        e_new = jnp.exp(s_new - m)
        e_scr[sl] = e.astype(BF16)
        return e_new.astype(BF16).astype(F32) * cn, jnp.sum(e, axis=-1, keepdims=True) + e_new

    def values(sl, c):
        keys = chunk_keys(c)
        return jnp.dot(e_scr[sl, :, keys], kb[sl, keys, :], preferred_element_type=F32)

    @pl.when(g == 0)
    def _():
        for p in range(n_pages):
            start_page(0, 0, p)

    n_chunks = n_pages // chunk
    first = 2 * g
    after = jnp.minimum(first + 2, n_seq - 1)
    wait_gather(first, 0)
    for c in range(n_chunks):
        score_chunk(0, c, first + 1)
    acc, den = softmax(0)
    wait_gather(first + 1, 1)
    for c in range(n_chunks):
        score_chunk(1, c, after)
        acc = acc + values(0, c)
    o_ref[0] = acc / den
    acc, den = softmax(1)
    for c in range(n_chunks):
        acc = acc + values(1, c)
    o_ref[1] = acc / den

    @pl.when(g == pl.num_programs(0) - 1)
    def _():
        wait_gather(n_seq - 1, 0)


def _paged_attention(page_table, qlat3, qr3, ckv_new, kpe_new, cache_ckv, cache_kpe_t, layer,
                     chunk):
    nb, n_pages = page_table.shape
    n_keys = n_pages * PAGE_SIZE
    pair = lambda shape: pl.BlockSpec((2,) + shape, lambda g, pt: (g, 0, 0))
    hbm = pl.BlockSpec(memory_space=pl.ANY)
    grid_spec = pltpu.PrefetchScalarGridSpec(
        num_scalar_prefetch=1,
        grid=(nb // 2,),
        in_specs=[pair((MLA_HEADS, KV_LORA)), pair((MLA_HEADS, QK_ROPE)),
                  pair((1, KV_LORA)), pair((1, QK_ROPE)), hbm, hbm],
        out_specs=pair((MLA_HEADS, KV_LORA)),
        scratch_shapes=[
            pltpu.VMEM((n_keys, KV_LORA), F32),
            pltpu.VMEM((n_keys, KV_LORA), F32),
            pltpu.VMEM((n_pages, QK_ROPE, PAGE_SIZE), F32),
            pltpu.VMEM((n_pages, QK_ROPE, PAGE_SIZE), F32),
            pltpu.VMEM((2, n_keys, KV_LORA), BF16),
            pltpu.VMEM((QK_ROPE, n_keys), BF16),
            pltpu.VMEM((MLA_HEADS, n_keys), F32),
            pltpu.VMEM((2, MLA_HEADS, n_keys), BF16),
            pltpu.SemaphoreType.DMA((2, 2)),
        ],
    )
    return pl.pallas_call(
        functools.partial(_paged_kernel, layer=layer, n_pages=n_pages, chunk=chunk),
        grid_spec=grid_spec,
        out_shape=jax.ShapeDtypeStruct((nb, MLA_HEADS, KV_LORA), F32),
        compiler_params=_params("arbitrary"),
        name="mla_paged",
    )(page_table, qlat3, qr3, ckv_new, kpe_new, cache_ckv, cache_kpe_t)


def _olat_kernel(ol_ref, wuv_ref, sg_ref, o_ref):
    o = jnp.dot(ol_ref[...].astype(BF16), wuv_ref[...], preferred_element_type=F32)
    o_ref[...] = (o * sg_ref[...]).astype(BF16)


def _olat_proj(olat, wuv3, sg):
    rows = olat.shape[0]
    return pl.pallas_call(
        _olat_kernel,
        grid=(MLA_HEADS,),
        in_specs=[pl.BlockSpec((rows, KV_LORA), lambda h: (0, h)),
                  pl.BlockSpec((None, KV_LORA, V_HEAD), lambda h: (h, 0, 0)),
                  pl.BlockSpec((rows, V_HEAD), lambda h: (0, h))],
        out_specs=pl.BlockSpec((rows, V_HEAD), lambda h: (0, h)),
        out_shape=jax.ShapeDtypeStruct((rows, MLA_WIDTH), BF16),
        compiler_params=_params("parallel"),
        name="mla_olat",
    )(olat, wuv3, sg)


def _swa_in_kernel(h_ref, g_ref, w_ref, q_ref, k_ref, v_ref, sg_ref):
    a = _rms(h_ref[...], g_ref[...]).astype(BF16)
    z = jnp.dot(a, w_ref[...], preferred_element_type=F32)
    q_ref[...] = (z[:, :SWA_Q] * SWA_SCALE).astype(BF16)
    k_ref[...] = z[:, SWA_Q:SWA_Q + SWA_KV]
    v_ref[...] = z[:, SWA_Q + SWA_KV:SWA_Q + 2 * SWA_KV]
    sg_ref[...] = _silu(z[:, SWA_Q + 2 * SWA_KV:])


def _swa_in(h, g, w, tm):
    rows = h.shape[0]
    row = lambda r: (r, 0)
    outs = (jax.ShapeDtypeStruct((rows, SWA_Q), BF16),
            jax.ShapeDtypeStruct((rows, SWA_KV), F32),
            jax.ShapeDtypeStruct((rows, SWA_KV), F32),
            jax.ShapeDtypeStruct((rows, SWA_Q), F32))
    return pl.pallas_call(
        _swa_in_kernel,
        grid=(rows // tm,),
        in_specs=[pl.BlockSpec((tm, D_MODEL), row), _full((1, D_MODEL)), _full(w.shape)],
        out_specs=tuple(pl.BlockSpec((tm, o.shape[1]), row) for o in outs),
        out_shape=outs,
        compiler_params=_params("parallel"),
        name="swa_in",
    )(h, g, w)


def _sink_softmax(s, sink):
    m = jnp.maximum(jnp.max(s, axis=-1, keepdims=True), sink)
    e = jnp.exp(s - m)
    return e / (jnp.sum(e, axis=-1, keepdims=True) + jnp.exp(sink - m))


def _alibi_slope(head):
    return 2.0 ** (-8.0 * (head + 1) / SWA_HEADS)


def _swa_bias():
    dist = (WINDOW + np.arange(WINDOW))[None, :] - np.arange(2 * WINDOW)[:, None]
    in_window = (dist >= 0) & (dist < WINDOW)
    slopes = np.asarray([_alibi_slope(h) for h in range(SWA_HEADS)], np.float32)
    bias = -slopes[:, None, None] * dist[None].astype(np.float32)
    return np.where(in_window[None], bias, np.float32(NEG_INF)).astype(np.float32)


def _swa_prompt_kernel(sink_ref, q_ref, kp_ref, kc_ref, vp_ref, vc_ref, sg_ref, bias_ref, o_ref):
    n = pl.program_id(1)
    kk = jnp.concatenate([kp_ref[...], kc_ref[...]], axis=0).astype(BF16)
    vvt = jnp.concatenate([vp_ref[...], vc_ref[...]], axis=0).T.astype(BF16)
    k_low = lax.broadcasted_iota(jnp.int32, (2 * WINDOW, LANES), 1) < SWA_HEAD_DIM
    v_low = lax.broadcasted_iota(jnp.int32, (LANES, 2 * WINDOW), 0) < SWA_HEAD_DIM
    zero = jnp.zeros((), BF16)
    prev_pen = jnp.where(n > 0, 0.0, NEG_INF).astype(F32)
    span = lambda g: slice(g * WINDOW, (g + 1) * WINDOW)
    pairs = range(SWA_KV_HEADS // 2)
    kv_span = [slice(pair * LANES, (pair + 1) * LANES) for pair in pairs]
    grp_lanes = [[slice((pair * SWA_GROUP + g) * LANES, (pair * SWA_GROUP + g + 1) * LANES)
                  for g in range(SWA_GROUP)] for pair in pairs]
    st_all = {}
    for pair in pairs:
        q_st = jnp.concatenate([q_ref[:, lanes] for lanes in grp_lanes[pair]], axis=0)
        for par in range(2):
            keep_k = k_low if par == 0 else ~k_low
            st_all[pair, par] = lax.dot_general(
                jnp.where(keep_k, kk[:, kv_span[pair]], zero), q_st, NT_DIMS,
                preferred_element_type=F32)
    for pair in pairs:
        vt_g = vvt[kv_span[pair], :]
        ot_pair = jnp.zeros((LANES, SWA_GROUP * WINDOW), F32)
        for par in range(2):
            keep_v = v_low if par == 0 else ~v_low
            es, inv_dens = [], []
            for g in range(SWA_GROUP):
                head = SWA_HEAD_ORDER[2 * (pair * SWA_GROUP + g) + par]
                s = st_all[pair, par][:, span(g)] + bias_ref[head]
                s_prev = s[:WINDOW, :] + prev_pen
                s_cur = s[WINDOW:, :]
                sink = sink_ref[head]
                m = jnp.maximum(jnp.maximum(jnp.max(s_prev, axis=0, keepdims=True),
                                            jnp.max(s_cur, axis=0, keepdims=True)), sink)
                e_prev = jnp.exp(s_prev - m)
                e_cur = jnp.exp(s_cur - m)
                den = (jnp.sum(e_prev, axis=0, keepdims=True)
                       + jnp.sum(e_cur, axis=0, keepdims=True) + jnp.exp(sink - m))
                es.append(jnp.concatenate([e_prev, e_cur], axis=0).astype(BF16))
                inv_dens.append(1.0 / den)
            ot_par = jnp.dot(jnp.where(keep_v, vt_g, zero), jnp.concatenate(es, axis=1),
                             preferred_element_type=F32)
            ot_pair = ot_pair + ot_par * jnp.concatenate(inv_dens, axis=1)
        o_pair = ot_pair.T
        for g, lanes in enumerate(grp_lanes[pair]):
            o_ref[:, lanes] = (o_pair[span(g), :] * sg_ref[:, lanes]).astype(BF16)


def _swa_prompt(sinks, q, k, v, sg):
    b, s, _ = q.shape
    cur = lambda bi, n: (bi, n, 0)
    prev = lambda bi, n: (bi, jnp.maximum(n - 1, 0), 0)
    wide = lambda idx: pl.BlockSpec((None, WINDOW, SWA_Q), idx)
    narrow = lambda idx: pl.BlockSpec((None, WINDOW, SWA_KV), idx)
    bias = _swa_bias()
    return pl.pallas_call(
        _swa_prompt_kernel,
        grid=(b, s // WINDOW),
        in_specs=[pl.BlockSpec(memory_space=pltpu.SMEM),
                  wide(cur), narrow(prev), narrow(cur), narrow(prev), narrow(cur), wide(cur),
                  _full(bias.shape)],
        out_specs=wide(cur),
        out_shape=jax.ShapeDtypeStruct((b, s, SWA_Q), BF16),
        compiler_params=_params("parallel", "parallel"),
        name="swa_prompt",
    )(sinks, q, k, k, v, v, sg, jnp.asarray(bias))


def _swa_sample_kernel(q_ref, bk_ref, bv_ref, kn_ref, vn_ref, sg_ref, sink_ref, slope_ref,
                       mask_ref, nk_ref, nv_ref, o_ref):
    n_seq = q_ref.shape[0]
    mask = mask_ref[...] > 0.5
    scores = []
    for j in range(n_seq):
        nk_ref[j, 0:WINDOW - 1, :] = bk_ref[j, 1:WINDOW, :]
        nk_ref[j, WINDOW - 1:WINDOW, :] = kn_ref[j]
        nv_ref[j, 0:WINDOW - 1, :] = bv_ref[j, 1:WINDOW, :]
        nv_ref[j, WINDOW - 1:WINDOW, :] = vn_ref[j]
        q_exp = jnp.where(mask, jnp.concatenate([q_ref[j]] * SWA_KV_HEADS, axis=-1),
                          jnp.zeros((), BF16))
        scores.append(lax.dot_general(q_exp, nk_ref[j].astype(BF16), NT_DIMS,
                                      preferred_element_type=F32))
    dist = (WINDOW - 1 - lax.broadcasted_iota(jnp.int32, scores[0].shape, 1)).astype(F32)
    bias = slope_ref[...] * dist
    probs = [_sink_softmax(s - bias, sink_ref[...]).astype(BF16) for s in scores]
    for j in range(n_seq):
        o_all = jnp.where(mask, jnp.dot(probs[j], nv_ref[j].astype(BF16),
                                        preferred_element_type=F32), 0.0)
        o = o_all[:, 0:SWA_HEAD_DIM]
        for kh in range(1, SWA_KV_HEADS):
            o = o + o_all[:, kh * SWA_HEAD_DIM:(kh + 1) * SWA_HEAD_DIM]
        o_ref[j] = (o * sg_ref[j]).astype(BF16)


def _swa_sample(q3, buf_k, buf_v, k_new, v_new, sg3, sink_col, slope_col, mask, layer, per_step):
    nb = q3.shape[0]
    per_b = lambda shape: pl.BlockSpec((per_step,) + shape, lambda b: (b, 0, 0))
    buf = pl.BlockSpec((None, per_step, WINDOW, SWA_KV), lambda b: (layer, b, 0, 0))
    return pl.pallas_call(
        _swa_sample_kernel,
        grid=(nb // per_step,),
        in_specs=[per_b((SWA_HEADS, SWA_HEAD_DIM)), buf, buf,
                  per_b((1, SWA_KV)), per_b((1, SWA_KV)), per_b((SWA_HEADS, SWA_HEAD_DIM)),
                  _full((SWA_HEADS, 1)), _full((SWA_HEADS, 1)), _full((SWA_HEADS, SWA_KV))],
        out_specs=(per_b((WINDOW, SWA_KV)), per_b((WINDOW, SWA_KV)),
                   per_b((SWA_HEADS, SWA_HEAD_DIM))),
        out_shape=(jax.ShapeDtypeStruct((nb, WINDOW, SWA_KV), F32),
                   jax.ShapeDtypeStruct((nb, WINDOW, SWA_KV), F32),
                   jax.ShapeDtypeStruct((nb, SWA_HEADS, SWA_HEAD_DIM), BF16)),
        compiler_params=_params("parallel"),
        name="swa_sample",
    )(q3, buf_k, buf_v, k_new, v_new, sg3, sink_col, slope_col, mask)


def _rot_cols(w):
    half = QK_ROPE // 2
    return jnp.concatenate([-w[..., half:], w[..., :half]], axis=-1)


def _rope_tables(pos):
    inv = ROPE_THETA ** (-jnp.arange(0, QK_ROPE, 2, dtype=F32) / QK_ROPE)
    ang = pos.astype(F32)[:, None] * inv[None, :]
    ang = jnp.concatenate([ang, ang], axis=-1)
    cos, sin = jnp.cos(ang), jnp.sin(ang)
    return (jnp.concatenate([cos, sin], axis=-1), jnp.concatenate([cos, cos], axis=-1),
            jnp.concatenate([sin, sin], axis=-1))


def _mla_weights(w_in, w_qb, w_uk, w_uv, w_o):
    g0 = Q_LORA + KV_LORA
    w_kpe = w_in[:, g0:g0 + QK_ROPE]
    w_in_r = jnp.concatenate(
        [w_in[:, :g0], w_in[:, g0 + QK_ROPE:], w_kpe, _rot_cols(w_kpe)], axis=1).astype(BF16)
    wqn = w_qb[:, :, :QK_NOPE].reshape(Q_LORA, MLA_HEADS * QK_NOPE).astype(BF16)
    wqr3 = w_qb[:, :, QK_NOPE:]
    wqr = wqr3.reshape(Q_LORA, MLA_HEADS * QK_ROPE).astype(BF16)
    wqrot = _rot_cols(wqr3).reshape(Q_LORA, MLA_HEADS * QK_ROPE).astype(BF16)
    wuk = w_uk.reshape(KV_LORA, MLA_HEADS * QK_NOPE).astype(BF16)
    wuvt = w_uv.reshape(KV_LORA, MLA_HEADS * V_HEAD).T.astype(BF16)
    wukt = jnp.transpose(w_uk, (1, 2, 0)).astype(BF16)
    wuv3 = jnp.transpose(w_uv, (1, 0, 2)).astype(BF16)
    return w_in_r, wqn, wqr, wqrot, wuk, wuvt, wukt, wuv3, w_o.astype(BF16)


def _swa_weights(w_in, w_o):
    cols = np.concatenate([np.arange(h * SWA_HEAD_DIM, (h + 1) * SWA_HEAD_DIM)
                           for h in SWA_HEAD_ORDER])
    g0 = SWA_Q + 2 * SWA_KV
    w_in_r = jnp.concatenate([w_in[:, :SWA_Q][:, cols], w_in[:, SWA_Q:g0], w_in[:, g0:][:, cols]],
                             axis=1).astype(BF16)
    return w_in_r, w_o[cols, :].astype(BF16)


def kernel(x_prompt, x_sample, cache_ckv, cache_kpe, state_swa_k, state_swa_v, page_table,
           pre_norm, post_norm, mla_w_in, mla_q_norm, mla_w_qb, mla_kv_norm, mla_w_uk,
           mla_w_uv, mla_w_o, swa_w_in, swa_sinks, swa_w_o):
    n_b, seq, _ = x_prompt.shape
    n_db, t_new, _ = x_sample.shape
    assert t_new == 1 and seq % 512 == 0 and state_swa_k.shape[2] == WINDOW
    depth = pre_norm.shape[0]
    past_len = page_table.shape[1] * PAGE_SIZE

    cs_p, cc_p, ss_p = _rope_tables(jnp.arange(seq, dtype=jnp.int32))
    cs_s, cc_s, ss_s = (jnp.broadcast_to(t, (n_db, LANES))
                        for t in _rope_tables(jnp.full((1,), past_len, jnp.int32)))

    order = np.asarray(SWA_HEAD_ORDER)
    slope_col = jnp.asarray([[_alibi_slope(h)] for h in SWA_HEAD_ORDER], F32)
    lane_kv = np.arange(SWA_KV)[None, :] // SWA_HEAD_DIM
    mask = jnp.asarray(lane_kv == (order // SWA_GROUP)[:, None], F32)
    cache_kpe_t = jnp.swapaxes(cache_kpe, 2, 3)
    buf_k = state_swa_k.reshape(state_swa_k.shape[:3] + (SWA_KV,))
    buf_v = state_swa_v.reshape(state_swa_v.shape[:3] + (SWA_KV,))

    hp = x_prompt.reshape(n_b * seq, D_MODEL)
    hs = x_sample.reshape(n_db, D_MODEL)
    ckv_p, kpe_p, ckv_s, kpe_s = [], [], [], []
    swk_p, swv_p, swk_s, swv_s = [], [], [], []
    for i in range(depth):
        l = i // 2
        g_pre = pre_norm[i][None, :]
        g_post = post_norm[i][None, :]
        if i % 2 == 0:
            w_in, wqn, wqr, wqrot, wuk, wuvt, wukt, wuv3, w_o = _mla_weights(
                mla_w_in[l], mla_w_qb[l], mla_w_uk[l], mla_w_uv[l], mla_w_o[l])
            qn_g = mla_q_norm[l][None, :]
            kvn_g = mla_kv_norm[l][None, :]
            cq, ckv, ckvb, kpe2, kpeb, sg = _mla_in(hp, g_pre, w_in, qn_g, kvn_g, cs_p, 512)
            qn, qr2, kn, vt = _mla_qkv(cq, ckvb, wqn, wqr, wqrot, wuk, wuvt, cc_p, ss_p, 256)
            to3 = lambda a: a.reshape(n_b, seq, a.shape[-1])
            og = _mla_flash(to3(qn), to3(qr2), to3(kn), to3(kpeb), vt, to3(sg), 512, 256, 4)
            hp = _out_proj(og.reshape(n_b * seq, MLA_WIDTH), w_o, g_post, hp, 512)
            ckv_p.append(ckv.reshape(n_b, seq, KV_LORA))
            kpe_p.append(kpe2[:, :QK_ROPE].reshape(n_b, seq, QK_ROPE))
            cq, ckv, _, kpe2, _, sg = _mla_in(hs, g_pre, w_in, qn_g, kvn_g, cs_s, n_db)
            qn, qr = _mla_q(cq, wqn, wqr, wqrot, cc_s, ss_s)
            qlat = _qlat(qn, wukt)
            kpe_new = kpe2[:, :QK_ROPE]
            olat = _paged_attention(
                page_table, qlat.reshape(n_db, MLA_HEADS, KV_LORA),
                qr.reshape(n_db, MLA_HEADS, QK_ROPE), ckv.reshape(n_db, 1, KV_LORA),
                kpe_new.reshape(n_db, 1, QK_ROPE), cache_ckv, cache_kpe_t, l, 16)
            og = _olat_proj(olat.reshape(n_db, MLA_HEADS * KV_LORA), wuv3, sg)
            hs = _out_proj(og, w_o, g_post, hs, n_db)
            ckv_s.append(ckv.reshape(n_db, 1, KV_LORA))
            kpe_s.append(kpe_new.reshape(n_db, 1, QK_ROPE))
        else:
            w_in, w_o = _swa_weights(swa_w_in[l], swa_w_o[l])
            sinks = swa_sinks[l]
            q, k, v, sg = _swa_in(hp, g_pre, w_in, 512)
            to3 = lambda a: a.reshape(n_b, seq, a.shape[-1])
            og = _swa_prompt(sinks, to3(q), to3(k), to3(v), to3(sg))
            hp = _out_proj(og.reshape(n_b * seq, SWA_Q), w_o, g_post, hp, 512)
            keep = min(WINDOW, seq)
            tail = lambda a: to3(a)[:, seq - keep:].reshape(n_b, keep, SWA_KV_HEADS, SWA_HEAD_DIM)
            swk_p.append(tail(k))
            swv_p.append(tail(v))
            q, k, v, sg = _swa_in(hs, g_pre, w_in, n_db)
            nk, nv, og3 = _swa_sample(
                q.reshape(n_db, SWA_HEADS, SWA_HEAD_DIM), buf_k, buf_v,
                k.reshape(n_db, 1, SWA_KV), v.reshape(n_db, 1, SWA_KV),
                sg.reshape(n_db, SWA_HEADS, SWA_HEAD_DIM),
                sinks[order][:, None], slope_col, mask, l, 8)
            hs = _out_proj(og3.reshape(n_db, SWA_Q), w_o, g_post, hs, n_db)
            swk_s.append(nk.reshape(n_db, WINDOW, SWA_KV_HEADS, SWA_HEAD_DIM))
            swv_s.append(nv.reshape(n_db, WINDOW, SWA_KV_HEADS, SWA_HEAD_DIM))
    return (hp.reshape(n_b, seq, D_MODEL), hs.reshape(n_db, t_new, D_MODEL),
            jnp.stack(ckv_p), jnp.stack(kpe_p), jnp.stack(ckv_s), jnp.stack(kpe_s),
            jnp.stack(swk_p), jnp.stack(swv_p), jnp.stack(swk_s), jnp.stack(swv_s))
```

```python
import functools
import math

import jax
import jax.numpy as jnp
import numpy as np
from jax import lax
from jax.experimental import pallas as pl
from jax.experimental.pallas import tpu as pltpu

F32 = jnp.float32
BF16 = jnp.bfloat16

D_MODEL = 1024
NORM_EPS = 1e-6
NEG_INF = -1e30
PAGE_SIZE = 128

MLA_HEADS = 16
Q_LORA = 512
KV_LORA = 256
QK_NOPE = 128
QK_ROPE = 64
V_HEAD = 128
ROPE_THETA = 10000.0
MLA_WIDTH = MLA_HEADS * V_HEAD
MLA_SCALE = 1.0 / math.sqrt(QK_NOPE + QK_ROPE)
MLA_QSCALE = MLA_SCALE * math.log2(math.e)

SWA_HEADS = 16
SWA_KV_HEADS = 4
SWA_GROUP = SWA_HEADS // SWA_KV_HEADS
SWA_HEAD_DIM = 64
WINDOW = 128
SWA_Q = SWA_HEADS * SWA_HEAD_DIM
SWA_KV = SWA_KV_HEADS * SWA_HEAD_DIM
SWA_SCALE = 1.0 / math.sqrt(SWA_HEAD_DIM)

LANES = 128
VMEM_LIMIT = 56 * 1024 * 1024

ROW_TILE = 512
QKV_ROW_TILE = 256
FLASH_Q_BLOCK = 512
FLASH_KEY_CHUNK = 256
FLASH_HEADS = 4
PAGED_CHUNK_PAGES = 16
SWA_SAMPLE_SEQS = 8

SWA_HEAD_ORDER = tuple(
    (2 * i + par) * SWA_GROUP + g
    for i in range(SWA_KV_HEADS // 2) for g in range(SWA_GROUP) for par in range(2))

NT_DIMS = (((1,), (1,)), ((), ()))


def _params(*sem):
    return pltpu.CompilerParams(dimension_semantics=sem, vmem_limit_bytes=VMEM_LIMIT)


def _rms(x, g):
    return x * lax.rsqrt(jnp.mean(x * x, axis=-1, keepdims=True) + NORM_EPS) * g


def _silu(x):
    return x * (1.0 / (1.0 + jnp.exp(-x)))


def _full(shape):
    return pl.BlockSpec(shape, lambda *_: (0,) * len(shape))


def _of_layer(stacked, layer):
    shape = stacked.shape[1:]
    return pl.BlockSpec((None,) + shape, lambda *_: (layer,) + (0,) * len(shape))


def _mla_in_kernel(h_ref, g_ref, w_ref, qn_ref, kvn_ref, cs_ref,
                   cq_ref, ckv_ref, ckvb_ref, kpe_ref, kpeb_ref, sg_ref):
    a = _rms(h_ref[...], g_ref[...]).astype(BF16)
    z = jnp.dot(a, w_ref[...], preferred_element_type=F32)
    cq_ref[...] = _rms(z[:, :Q_LORA], qn_ref[...]).astype(BF16)
    ckv = _rms(z[:, Q_LORA:Q_LORA + KV_LORA], kvn_ref[...])
    ckv_ref[...] = ckv
    ckvb_ref[...] = ckv.astype(BF16)
    g0 = Q_LORA + KV_LORA
    sg_ref[...] = _silu(z[:, g0:g0 + MLA_WIDTH])
    t = z[:, g0 + MLA_WIDTH:] * cs_ref[...]
    kpe2 = t + pltpu.roll(t, QK_ROPE, axis=1)
    kpe_ref[...] = kpe2
    kpeb_ref[...] = kpe2.astype(BF16)


def _mla_in(h, gains, w, qn, kvn, cs, tm, depth_i, layer):
    rows = h.shape[0]
    n_cs = cs.shape[0] // tm
    row = lambda r: (r, 0)
    outs = (
        jax.ShapeDtypeStruct((rows, Q_LORA), BF16),
        jax.ShapeDtypeStruct((rows, KV_LORA), F32),
        jax.ShapeDtypeStruct((rows, KV_LORA), BF16),
        jax.ShapeDtypeStruct((rows, LANES), F32),
        jax.ShapeDtypeStruct((rows, LANES), BF16),
        jax.ShapeDtypeStruct((rows, MLA_WIDTH), F32),
    )
    return pl.pallas_call(
        _mla_in_kernel,
        grid=(rows // tm,),
        in_specs=[
            pl.BlockSpec((tm, D_MODEL), row),
            _of_layer(gains, depth_i), _of_layer(w, layer),
            _of_layer(qn, layer), _of_layer(kvn, layer),
            pl.BlockSpec((tm, LANES), lambda r: (r % n_cs, 0)),
        ],
        out_specs=tuple(pl.BlockSpec((tm, o.shape[1]), row) for o in outs),
        out_shape=outs,
        compiler_params=_params("parallel"),
        name="mla_in",
    )(h, gains, w, qn, kvn, cs)


def _rope_groups(cq, wr_ref, wrot_ref, cc, ss):
    r = jnp.dot(cq, wr_ref[...], preferred_element_type=F32)
    rr = jnp.dot(cq, wrot_ref[...], preferred_element_type=F32)
    n_grp = r.shape[1] // LANES
    return [r[:, j * LANES:(j + 1) * LANES] * cc + rr[:, j * LANES:(j + 1) * LANES] * ss
            for j in range(n_grp)]


def _mla_qkv_kernel(cq_ref, ckvb_ref, wqn_ref, wqr_ref, wqrot_ref, wuk_ref, wuvt_ref,
                    cc_ref, ss_ref, qn_ref, qr2_ref, kn_ref, vt_ref):
    cq = cq_ref[...]
    qn = jnp.dot(cq, wqn_ref[...], preferred_element_type=F32)
    qn_ref[...] = (qn * MLA_QSCALE).astype(BF16)
    groups = _rope_groups(cq, wqr_ref, wqrot_ref, cc_ref[...], ss_ref[...])
    low = lax.broadcasted_iota(jnp.int32, groups[0].shape, 1) < QK_ROPE
    for j, grp in enumerate(groups):
        grp = grp * MLA_QSCALE
        qr2_ref[:, (2 * j) * LANES:(2 * j + 1) * LANES] = jnp.where(low, grp, 0.0).astype(BF16)
        qr2_ref[:, (2 * j + 1) * LANES:(2 * j + 2) * LANES] = jnp.where(low, 0.0, grp).astype(BF16)
    ckv = ckvb_ref[...]
    kn_ref[...] = jnp.dot(ckv, wuk_ref[...], preferred_element_type=F32).astype(BF16)
    vt_ref[...] = lax.dot_general(wuvt_ref[...], ckv, NT_DIMS,
                                  preferred_element_type=F32).astype(BF16)


def _mla_qkv(cq, ckvb, wqn, wqr, wqrot, wuk, wuvt, cc, ss, tm, layer):
    rows = cq.shape[0]
    n_cs = cc.shape[0] // tm
    row = lambda r: (r, 0)
    tab = pl.BlockSpec((tm, LANES), lambda r: (r % n_cs, 0))
    wide = jax.ShapeDtypeStruct((rows, MLA_WIDTH), BF16)
    return pl.pallas_call(
        _mla_qkv_kernel,
        grid=(rows // tm,),
        in_specs=[
            pl.BlockSpec((tm, Q_LORA), row),
            pl.BlockSpec((tm, KV_LORA), row),
            _of_layer(wqn, layer), _of_layer(wqr, layer), _of_layer(wqrot, layer),
            _of_layer(wuk, layer), _of_layer(wuvt, layer), tab, tab,
        ],
        out_specs=(pl.BlockSpec((tm, MLA_WIDTH), row),) * 3
        + (pl.BlockSpec((MLA_WIDTH, tm), lambda r: (0, r)),),
        out_shape=(wide,) * 3 + (jax.ShapeDtypeStruct((MLA_WIDTH, rows), BF16),),
        compiler_params=_params("parallel"),
        name="mla_qkv",
    )(cq, ckvb, wqn, wqr, wqrot, wuk, wuvt, cc, ss)


def _mla_q_kernel(cq_ref, wqn_ref, wqr_ref, wqrot_ref, cc_ref, ss_ref, qn_ref, qr_ref):
    cq = cq_ref[...]
    qn_ref[...] = jnp.dot(cq, wqn_ref[...], preferred_element_type=F32).astype(BF16)
    groups = _rope_groups(cq, wqr_ref, wqrot_ref, cc_ref[...], ss_ref[...])
    for j, grp in enumerate(groups):
        qr_ref[:, j * LANES:(j + 1) * LANES] = grp.astype(BF16)


def _mla_q(cq, wqn, wqr, wqrot, cc, ss, layer):
    rows = cq.shape[0]
    return pl.pallas_call(
        _mla_q_kernel,
        grid=(1,),
        in_specs=[_full(cq.shape), _of_layer(wqn, layer), _of_layer(wqr, layer),
                  _of_layer(wqrot, layer), _full(cc.shape), _full(ss.shape)],
        out_specs=(_full((rows, MLA_WIDTH)), _full((rows, MLA_HEADS * QK_ROPE))),
        out_shape=(jax.ShapeDtypeStruct((rows, MLA_WIDTH), BF16),
                   jax.ShapeDtypeStruct((rows, MLA_HEADS * QK_ROPE), BF16)),
        compiler_params=_params("arbitrary"),
        name="mla_q_sample",
    )(cq, wqn, wqr, wqrot, cc, ss)


def _mla_flash_kernel(qn_ref, qr_ref, kn_ref, kpe_ref, vt_ref, sg_ref, o_ref, s_scr, mx_scr,
                      acc_scr, *, blk, sub, heads):
    i = pl.program_id(2)
    lanes = [slice(h * LANES, (h + 1) * LANES) for h in range(heads)]
    q = [jnp.concatenate([qn_ref[:, ln], qr_ref[:, ln]], axis=-1) for ln in lanes]

    def scores(h, start, col0, slot, diagonal):
        rows = pl.ds(start, sub)
        k = jnp.concatenate([kn_ref[rows, lanes[h]], kpe_ref[rows, :]], axis=-1)
        s = lax.dot_general(k, q[h][col0:, :], NT_DIMS, preferred_element_type=F32)
        if diagonal:
            s = causal(s)
        s_scr[h, slot, :, col0:] = s
        mx_scr[h, slot, :, col0:] = jnp.max(s, axis=0, keepdims=True)

    def causal(s):
        ki = lax.broadcasted_iota(jnp.int32, s.shape, 0)
        qi = lax.broadcasted_iota(jnp.int32, s.shape, 1)
        return jnp.where(ki <= qi, s, NEG_INF)

    def fold(h, slot, start, col0, carry, mask_now):
        m, l = (c[:, col0:] for c in carry)
        rows = pl.ds(start, sub)
        s = s_scr[h, slot, :, col0:]
        if mask_now:
            s = causal(s)
            m_new = jnp.maximum(m, jnp.max(s, axis=0, keepdims=True))
        else:
            m_new = jnp.maximum(m, mx_scr[h, slot, :, col0:])
        alpha = jnp.exp2(m - m_new)
        p = jnp.exp2(s - m_new)
        l = alpha * l + jnp.sum(p, axis=0, keepdims=True)
        acc_scr[h, :, col0:] = alpha * acc_scr[h, :, col0:] + jnp.dot(
            vt_ref[lanes[h], rows], p.astype(BF16), preferred_element_type=F32)
        if col0 == 0:
            return m_new, l
        return tuple(jnp.concatenate([c[:, :col0], n], axis=1)
                     for c, n in zip(carry, (m_new, l)))

    n_sub = blk // sub

    def full_block(j, carries):
        base = pl.multiple_of(j * blk, blk)
        carries = list(carries)
        for u in range(n_sub):
            for h in range(heads):
                scores(h, base + (u + 1) * sub, 0, (u + 1) % 2, False)
            for h in range(heads):
                carries[h] = fold(h, u % 2, base + u * sub, 0, carries[h], False)
        return tuple(carries)

    init = (jnp.full((1, blk), NEG_INF, F32), jnp.zeros((1, blk), F32))
    acc_scr[...] = jnp.zeros(acc_scr.shape, F32)
    for h in range(heads):
        scores(h, 0, 0, 0, False)
    carries = list(lax.fori_loop(0, i, full_block, (init,) * heads))
    base = pl.multiple_of(i * blk, blk)
    for u in range(n_sub):
        if u + 1 < n_sub:
            for h in range(heads):
                scores(h, base + (u + 1) * sub, (u + 1) * sub, (u + 1) % 2, True)
        for h in range(heads):
            carries[h] = fold(h, u % 2, base + u * sub, u * sub, carries[h], u == 0)
    for h in range(heads):
        _, l = carries[h]
        o_ref[:, lanes[h]] = ((acc_scr[h] / l).T * sg_ref[:, lanes[h]]).astype(BF16)


def _mla_flash(qn, qr2, kn, kpeb, vt, sg, blk, sub, heads):
    b, s, _ = qn.shape
    width = heads * LANES
    q_spec = pl.BlockSpec((None, blk, width), lambda bi, h, i: (bi, i, h))
    return pl.pallas_call(
        functools.partial(_mla_flash_kernel, blk=blk, sub=sub, heads=heads),
        grid=(b, MLA_HEADS // heads, s // blk),
        in_specs=[q_spec, q_spec,
                  pl.BlockSpec((None, s, width), lambda bi, h, i: (bi, 0, h)),
                  pl.BlockSpec((None, s, LANES), lambda bi, h, i: (bi, 0, 0)),
                  pl.BlockSpec((heads * V_HEAD, s), lambda bi, h, i: (h, bi)),
                  q_spec],
        out_specs=q_spec,
        out_shape=jax.ShapeDtypeStruct((b, s, MLA_WIDTH), BF16),
        scratch_shapes=[pltpu.VMEM((heads, 2, sub, blk), F32),
                        pltpu.VMEM((heads, 2, 1, blk), F32),
                        pltpu.VMEM((heads, V_HEAD, blk), F32)],
        compiler_params=_params("parallel", "parallel", "arbitrary"),
        name="mla_flash",
    )(qn, qr2, kn, kpeb, vt, sg)


def _out_kernel(x_ref, w_ref, g_ref, h_ref, o_ref):
    y = jnp.dot(x_ref[...], w_ref[...], preferred_element_type=F32)
    o_ref[...] = h_ref[...] + _rms(y, g_ref[...])


def _out_proj(x, w, gains, h, tm, layer, depth_i):
    rows, width = x.shape
    row = lambda r: (r, 0)
    return pl.pallas_call(
        _out_kernel,
        grid=(rows // tm,),
        in_specs=[pl.BlockSpec((tm, width), row), _of_layer(w, layer), _of_layer(gains, depth_i),
                  pl.BlockSpec((tm, D_MODEL), row)],
        out_specs=pl.BlockSpec((tm, D_MODEL), row),
        out_shape=jax.ShapeDtypeStruct((rows, D_MODEL), F32),
        compiler_params=_params("parallel"),
        name="out_proj",
    )(x, w, gains, h)


def _qlat_kernel(qn_ref, wukt_ref, o_ref):
    o_ref[...] = jnp.dot(qn_ref[...], wukt_ref[...], preferred_element_type=F32).astype(BF16)


def _qlat(qn, wukt, layer):
    rows = qn.shape[0]
    return pl.pallas_call(
        _qlat_kernel,
        grid=(MLA_HEADS,),
        in_specs=[pl.BlockSpec((rows, QK_NOPE), lambda h: (0, h)),
                  pl.BlockSpec((None, None, QK_NOPE, KV_LORA), lambda h: (layer, h, 0, 0))],
        out_specs=pl.BlockSpec((rows, KV_LORA), lambda h: (0, h)),
        out_shape=jax.ShapeDtypeStruct((rows, MLA_HEADS * KV_LORA), BF16),
        compiler_params=_params("parallel"),
        name="mla_qlat",
    )(qn, wukt)


def _paged_kernel(pt_ref, ql_ref, qr_ref, cnew_ref, knew_ref, ckv_hbm, kpe_hbm, o_ref,
                  ckv0, ckv1, kpe0, kpe1, kb, pb, s_scr, e_scr, sem, *, layer, n_pages, chunk):
    g = pl.program_id(0)
    n_seq = 2 * pl.num_programs(0)
    bufs = ((ckv0, kpe0), (ckv1, kpe1))

    def page_copies(seq, sl, p):
        pid = pt_ref[seq, p]
        ckv_buf, kpe_buf = bufs[sl]
        return (pltpu.make_async_copy(ckv_hbm.at[layer, pid],
                                      ckv_buf.at[pl.ds(p * PAGE_SIZE, PAGE_SIZE), :],
                                      sem.at[0, sl]),
                pltpu.make_async_copy(kpe_hbm.at[layer, pid], kpe_buf.at[p], sem.at[1, sl]))

    def start_page(seq, sl, p):
        for i, copy in enumerate(page_copies(seq, sl, p)):
            copy.start(priority=(p + i) % 2)

    def wait_gather(seq, sl):
        for p in range(n_pages):
            for copy in page_copies(seq, sl, p):
                copy.wait()

    def chunk_keys(c):
        return slice(c * chunk * PAGE_SIZE, (c + 1) * chunk * PAGE_SIZE)

    def score_chunk(sl, c, nxt):
        ckv_buf, kpe_buf = bufs[sl]
        for p in range(c * chunk, (c + 1) * chunk):
            start_page(nxt, 1 - sl, p)
            keys = slice(p * PAGE_SIZE, (p + 1) * PAGE_SIZE)
            kb[sl, keys, :] = ckv_buf[keys, :].astype(BF16)
            pb[:, keys] = kpe_buf[p].astype(BF16)
        keys = chunk_keys(c)
        s_scr[:, keys] = (
            lax.dot_general(ql_ref[sl], kb[sl, keys, :], NT_DIMS, preferred_element_type=F32)
            + jnp.dot(qr_ref[sl], pb[:, keys], preferred_element_type=F32)) * MLA_SCALE

    def softmax(sl):
        s = s_scr[...]
        cn = cnew_ref[sl].astype(BF16).astype(F32)
        kn = knew_ref[sl].astype(BF16).astype(F32)
        s_new = (jnp.sum(ql_ref[sl].astype(F32) * cn, axis=-1, keepdims=True)
                 + jnp.sum(qr_ref[sl].astype(F32) * kn, axis=-1, keepdims=True)) * MLA_SCALE
        m = jnp.maximum(jnp.max(s, axis=-1, keepdims=True), s_new)
        e = jnp.exp(s - m)
        e_new = jnp.exp(s_new - m)
        e_scr[sl] = e.astype(BF16)
        return e_new.astype(BF16).astype(F32) * cn, jnp.sum(e, axis=-1, keepdims=True) + e_new

    def values(sl, c):
        keys = chunk_keys(c)
        return jnp.dot(e_scr[sl, :, keys], kb[sl, keys, :], preferred_element_type=F32)

    @pl.when(g == 0)
    def _():
        for p in range(n_pages):
            start_page(0, 0, p)

    n_chunks = n_pages // chunk
    first = 2 * g
    after = jnp.minimum(first + 2, n_seq - 1)
    wait_gather(first, 0)
    for c in range(n_chunks):
        score_chunk(0, c, first + 1)
    acc, den = softmax(0)
    wait_gather(first + 1, 1)
    for c in range(n_chunks):
        score_chunk(1, c, after)
        acc = acc + values(0, c)
    o_ref[0] = acc / den
    acc, den = softmax(1)
    for c in range(n_chunks):
        acc = acc + values(1, c)
    o_ref[1] = acc / den

    @pl.when(g == pl.num_programs(0) - 1)
    def _():
        wait_gather(n_seq - 1, 0)


def _paged_attention(page_table, qlat3, qr3, ckv_new, kpe_new, cache_ckv, cache_kpe_t, layer,
                     chunk):
    nb, n_pages = page_table.shape
    n_keys = n_pages * PAGE_SIZE
    pair = lambda shape: pl.BlockSpec((2,) + shape, lambda g, pt: (g, 0, 0))
    hbm = pl.BlockSpec(memory_space=pl.ANY)
    grid_spec = pltpu.PrefetchScalarGridSpec(
        num_scalar_prefetch=1,
        grid=(nb // 2,),
        in_specs=[pair((MLA_HEADS, KV_LORA)), pair((MLA_HEADS, QK_ROPE)),
                  pair((1, KV_LORA)), pair((1, QK_ROPE)), hbm, hbm],
        out_specs=pair((MLA_HEADS, KV_LORA)),
        scratch_shapes=[
            pltpu.VMEM((n_keys, KV_LORA), F32),
            pltpu.VMEM((n_keys, KV_LORA), F32),
            pltpu.VMEM((n_pages, QK_ROPE, PAGE_SIZE), F32),
            pltpu.VMEM((n_pages, QK_ROPE, PAGE_SIZE), F32),
            pltpu.VMEM((2, n_keys, KV_LORA), BF16),
            pltpu.VMEM((QK_ROPE, n_keys), BF16),
            pltpu.VMEM((MLA_HEADS, n_keys), F32),
            pltpu.VMEM((2, MLA_HEADS, n_keys), BF16),
            pltpu.SemaphoreType.DMA((2, 2)),
        ],
    )
    return pl.pallas_call(
        functools.partial(_paged_kernel, layer=layer, n_pages=n_pages, chunk=chunk),
        grid_spec=grid_spec,
        out_shape=jax.ShapeDtypeStruct((nb, MLA_HEADS, KV_LORA), F32),
        compiler_params=_params("arbitrary"),
        name="mla_paged",
    )(page_table, qlat3, qr3, ckv_new, kpe_new, cache_ckv, cache_kpe_t)


def _olat_kernel(ol_ref, wuv_ref, sg_ref, o_ref):
    o = jnp.dot(ol_ref[...].astype(BF16), wuv_ref[...], preferred_element_type=F32)
    o_ref[...] = (o * sg_ref[...]).astype(BF16)


def _olat_proj(olat, wuv3, sg, layer):
    rows = olat.shape[0]
    return pl.pallas_call(
        _olat_kernel,
        grid=(MLA_HEADS,),
        in_specs=[pl.BlockSpec((rows, KV_LORA), lambda h: (0, h)),
                  pl.BlockSpec((None, None, KV_LORA, V_HEAD), lambda h: (layer, h, 0, 0)),
                  pl.BlockSpec((rows, V_HEAD), lambda h: (0, h))],
        out_specs=pl.BlockSpec((rows, V_HEAD), lambda h: (0, h)),
        out_shape=jax.ShapeDtypeStruct((rows, MLA_WIDTH), BF16),
        compiler_params=_params("parallel"),
        name="mla_olat",
    )(olat, wuv3, sg)


def _swa_in_kernel(h_ref, g_ref, w_ref, q_ref, k_ref, v_ref, sg_ref):
    a = _rms(h_ref[...], g_ref[...]).astype(BF16)
    z = jnp.dot(a, w_ref[...], preferred_element_type=F32)
    q_ref[...] = (z[:, :SWA_Q] * SWA_SCALE).astype(BF16)
    k_ref[...] = z[:, SWA_Q:SWA_Q + SWA_KV]
    v_ref[...] = z[:, SWA_Q + SWA_KV:SWA_Q + 2 * SWA_KV]
    sg_ref[...] = _silu(z[:, SWA_Q + 2 * SWA_KV:])


def _swa_in(h, gains, w, tm, depth_i, layer):
    rows = h.shape[0]
    row = lambda r: (r, 0)
    outs = (jax.ShapeDtypeStruct((rows, SWA_Q), BF16),
            jax.ShapeDtypeStruct((rows, SWA_KV), F32),
            jax.ShapeDtypeStruct((rows, SWA_KV), F32),
            jax.ShapeDtypeStruct((rows, SWA_Q), F32))
    return pl.pallas_call(
        _swa_in_kernel,
        grid=(rows // tm,),
        in_specs=[pl.BlockSpec((tm, D_MODEL), row), _of_layer(gains, depth_i),
                  _of_layer(w, layer)],
        out_specs=tuple(pl.BlockSpec((tm, o.shape[1]), row) for o in outs),
        out_shape=outs,
        compiler_params=_params("parallel"),
        name="swa_in",
    )(h, gains, w)


def _sink_softmax(s, sink):
    m = jnp.maximum(jnp.max(s, axis=-1, keepdims=True), sink)
    e = jnp.exp(s - m)
    return e / (jnp.sum(e, axis=-1, keepdims=True) + jnp.exp(sink - m))


def _alibi_slope(head):
    return 2.0 ** (-8.0 * (head + 1) / SWA_HEADS)


def _swa_bias():
    dist = (WINDOW + np.arange(WINDOW))[None, :] - np.arange(2 * WINDOW)[:, None]
    in_window = (dist >= 0) & (dist < WINDOW)
    slopes = np.asarray([_alibi_slope(h) for h in range(SWA_HEADS)], np.float32)
    bias = -slopes[:, None, None] * dist[None].astype(np.float32)
    return np.where(in_window[None], bias, np.float32(NEG_INF)).astype(np.float32)


def _swa_prompt_kernel(sink_ref, q_ref, kp_ref, kc_ref, vp_ref, vc_ref, sg_ref, bias_ref, o_ref,
                       *, layer):
    n = pl.program_id(1)
    kk = jnp.concatenate([kp_ref[...], kc_ref[...]], axis=0).astype(BF16)
    vvt = jnp.concatenate([vp_ref[...], vc_ref[...]], axis=0).T.astype(BF16)
    k_low = lax.broadcasted_iota(jnp.int32, (2 * WINDOW, LANES), 1) < SWA_HEAD_DIM
    v_low = lax.broadcasted_iota(jnp.int32, (LANES, 2 * WINDOW), 0) < SWA_HEAD_DIM
    zero = jnp.zeros((), BF16)
    prev_pen = jnp.where(n > 0, 0.0, NEG_INF).astype(F32)
    span = lambda g: slice(g * WINDOW, (g + 1) * WINDOW)
    pairs = range(SWA_KV_HEADS // 2)
    kv_span = [slice(pair * LANES, (pair + 1) * LANES) for pair in pairs]
    grp_lanes = [[slice((pair * SWA_GROUP + g) * LANES, (pair * SWA_GROUP + g + 1) * LANES)
                  for g in range(SWA_GROUP)] for pair in pairs]
    st_all = {}
    for pair in pairs:
        q_st = jnp.concatenate([q_ref[:, lanes] for lanes in grp_lanes[pair]], axis=0)
        for par in range(2):
            keep_k = k_low if par == 0 else ~k_low
            st_all[pair, par] = lax.dot_general(
                jnp.where(keep_k, kk[:, kv_span[pair]], zero), q_st, NT_DIMS,
                preferred_element_type=F32)
    for pair in pairs:
        vt_g = vvt[kv_span[pair], :]
        ot_pair = jnp.zeros((LANES, SWA_GROUP * WINDOW), F32)
        for par in range(2):
            keep_v = v_low if par == 0 else ~v_low
            es, inv_dens = [], []
            for g in range(SWA_GROUP):
                head = SWA_HEAD_ORDER[2 * (pair * SWA_GROUP + g) + par]
                s = st_all[pair, par][:, span(g)] + bias_ref[head]
                s_prev = s[:WINDOW, :] + prev_pen
                s_cur = s[WINDOW:, :]
                sink = sink_ref[layer, head]
                m = jnp.maximum(jnp.maximum(jnp.max(s_prev, axis=0, keepdims=True),
                                            jnp.max(s_cur, axis=0, keepdims=True)), sink)
                e_prev = jnp.exp(s_prev - m)
                e_cur = jnp.exp(s_cur - m)
                den = (jnp.sum(e_prev, axis=0, keepdims=True)
                       + jnp.sum(e_cur, axis=0, keepdims=True) + jnp.exp(sink - m))
                es.append(jnp.concatenate([e_prev, e_cur], axis=0).astype(BF16))
                inv_dens.append(1.0 / den)
            ot_par = jnp.dot(jnp.where(keep_v, vt_g, zero), jnp.concatenate(es, axis=1),
                             preferred_element_type=F32)
            ot_pair = ot_pair + ot_par * jnp.concatenate(inv_dens, axis=1)
        o_pair = ot_pair.T
        for g, lanes in enumerate(grp_lanes[pair]):
            o_ref[:, lanes] = (o_pair[span(g), :] * sg_ref[:, lanes]).astype(BF16)


def _swa_prompt(sinks, q, k, v, sg, layer):
    b, s, _ = q.shape
    cur = lambda bi, n: (bi, n, 0)
    prev = lambda bi, n: (bi, jnp.maximum(n - 1, 0), 0)
    wide = lambda idx: pl.BlockSpec((None, WINDOW, SWA_Q), idx)
    narrow = lambda idx: pl.BlockSpec((None, WINDOW, SWA_KV), idx)
    bias = _swa_bias()
    return pl.pallas_call(
        functools.partial(_swa_prompt_kernel, layer=layer),
        grid=(b, s // WINDOW),
        in_specs=[pl.BlockSpec(memory_space=pltpu.SMEM),
                  wide(cur), narrow(prev), narrow(cur), narrow(prev), narrow(cur), wide(cur),
                  _full(bias.shape)],
        out_specs=wide(cur),
        out_shape=jax.ShapeDtypeStruct((b, s, SWA_Q), BF16),
        compiler_params=_params("parallel", "parallel"),
        name="swa_prompt",
    )(sinks, q, k, k, v, v, sg, jnp.asarray(bias))


def _swa_sample_kernel(q_ref, bk_ref, bv_ref, kn_ref, vn_ref, sg_ref, sink_ref, slope_ref,
                       mask_ref, nk_ref, nv_ref, o_ref):
    n_seq = q_ref.shape[0]
    mask = mask_ref[...] > 0.5
    scores = []
    for j in range(n_seq):
        nk_ref[j, 0:WINDOW - 1, :] = bk_ref[j, 1:WINDOW, :]
        nk_ref[j, WINDOW - 1:WINDOW, :] = kn_ref[j]
        nv_ref[j, 0:WINDOW - 1, :] = bv_ref[j, 1:WINDOW, :]
        nv_ref[j, WINDOW - 1:WINDOW, :] = vn_ref[j]
        q_exp = jnp.where(mask, jnp.concatenate([q_ref[j]] * SWA_KV_HEADS, axis=-1),
                          jnp.zeros((), BF16))
        scores.append(lax.dot_general(q_exp, nk_ref[j].astype(BF16), NT_DIMS,
                                      preferred_element_type=F32))
    dist = (WINDOW - 1 - lax.broadcasted_iota(jnp.int32, scores[0].shape, 1)).astype(F32)
    bias = slope_ref[...] * dist
    probs = [_sink_softmax(s - bias, sink_ref[...]).astype(BF16) for s in scores]
    for j in range(n_seq):
        o_all = jnp.where(mask, jnp.dot(probs[j], nv_ref[j].astype(BF16),
                                        preferred_element_type=F32), 0.0)
        o = o_all[:, 0:SWA_HEAD_DIM]
        for kh in range(1, SWA_KV_HEADS):
            o = o + o_all[:, kh * SWA_HEAD_DIM:(kh + 1) * SWA_HEAD_DIM]
        o_ref[j] = (o * sg_ref[j]).astype(BF16)


def _swa_sample(q3, buf_k, buf_v, k_new, v_new, sg3, sink_col, slope_col, mask, layer, per_step):
    nb = q3.shape[0]
    per_b = lambda shape: pl.BlockSpec((per_step,) + shape, lambda b: (b, 0, 0))
    buf = pl.BlockSpec((None, per_step, WINDOW, SWA_KV), lambda b: (layer, b, 0, 0))
    return pl.pallas_call(
        _swa_sample_kernel,
        grid=(nb // per_step,),
        in_specs=[per_b((SWA_HEADS, SWA_HEAD_DIM)), buf, buf,
                  per_b((1, SWA_KV)), per_b((1, SWA_KV)), per_b((SWA_HEADS, SWA_HEAD_DIM)),
                  _of_layer(sink_col, layer), _full((SWA_HEADS, 1)),
                  _full((SWA_HEADS, SWA_KV))],
        out_specs=(per_b((WINDOW, SWA_KV)), per_b((WINDOW, SWA_KV)),
                   per_b((SWA_HEADS, SWA_HEAD_DIM))),
        out_shape=(jax.ShapeDtypeStruct((nb, WINDOW, SWA_KV), F32),
                   jax.ShapeDtypeStruct((nb, WINDOW, SWA_KV), F32),
                   jax.ShapeDtypeStruct((nb, SWA_HEADS, SWA_HEAD_DIM), BF16)),
        compiler_params=_params("parallel"),
        name="swa_sample",
    )(q3, buf_k, buf_v, k_new, v_new, sg3, sink_col, slope_col, mask)


def _rot_cols(w):
    half = QK_ROPE // 2
    return jnp.concatenate([-w[..., half:], w[..., :half]], axis=-1)


def _rope_tables(pos):
    inv = ROPE_THETA ** (-jnp.arange(0, QK_ROPE, 2, dtype=F32) / QK_ROPE)
    ang = pos.astype(F32)[:, None] * inv[None, :]
    ang = jnp.concatenate([ang, ang], axis=-1)
    cos, sin = jnp.cos(ang), jnp.sin(ang)
    return (jnp.concatenate([cos, sin], axis=-1), jnp.concatenate([cos, cos], axis=-1),
            jnp.concatenate([sin, sin], axis=-1))


def _mla_weights(w_in, w_qb, w_uk, w_uv, w_o):
    n_l = w_in.shape[0]
    g0 = Q_LORA + KV_LORA
    w_kpe = w_in[..., g0:g0 + QK_ROPE]
    w_in_r = jnp.concatenate(
        [w_in[..., :g0], w_in[..., g0 + QK_ROPE:], w_kpe, _rot_cols(w_kpe)], axis=-1).astype(BF16)
    wqn = w_qb[..., :QK_NOPE].reshape(n_l, Q_LORA, MLA_HEADS * QK_NOPE).astype(BF16)
    wqr3 = w_qb[..., QK_NOPE:]
    wqr = wqr3.reshape(n_l, Q_LORA, MLA_HEADS * QK_ROPE).astype(BF16)
    wqrot = _rot_cols(wqr3).reshape(n_l, Q_LORA, MLA_HEADS * QK_ROPE).astype(BF16)
    wuk = w_uk.reshape(n_l, KV_LORA, MLA_HEADS * QK_NOPE).astype(BF16)
    wuvt = jnp.swapaxes(w_uv.reshape(n_l, KV_LORA, MLA_HEADS * V_HEAD), 1, 2).astype(BF16)
    wukt = jnp.transpose(w_uk, (0, 2, 3, 1)).astype(BF16)
    wuv3 = jnp.transpose(w_uv, (0, 2, 1, 3)).astype(BF16)
    return w_in_r, wqn, wqr, wqrot, wuk, wuvt, wukt, wuv3, w_o.astype(BF16)


def _swa_weights(w_in, w_o):
    cols = np.concatenate([np.arange(h * SWA_HEAD_DIM, (h + 1) * SWA_HEAD_DIM)
                           for h in SWA_HEAD_ORDER])
    g0 = SWA_Q + 2 * SWA_KV
    w_in_r = jnp.concatenate(
        [w_in[..., :SWA_Q][..., cols], w_in[..., SWA_Q:g0], w_in[..., g0:][..., cols]],
        axis=-1).astype(BF16)
    return w_in_r, w_o[:, cols, :].astype(BF16)


def kernel(x_prompt, x_sample, cache_ckv, cache_kpe, state_swa_k, state_swa_v, page_table,
           pre_norm, post_norm, mla_w_in, mla_q_norm, mla_w_qb, mla_kv_norm, mla_w_uk,
           mla_w_uv, mla_w_o, swa_w_in, swa_sinks, swa_w_o):
    n_b, seq, _ = x_prompt.shape
    n_db, t_new, _ = x_sample.shape
    n_pages = page_table.shape[1]
    assert t_new == 1 and state_swa_k.shape[2] == WINDOW
    assert seq % ROW_TILE == 0 and seq % FLASH_Q_BLOCK == 0
    assert n_db % 2 == 0 and n_db % SWA_SAMPLE_SEQS == 0 and n_pages % PAGED_CHUNK_PAGES == 0
    depth = pre_norm.shape[0]
    past_len = n_pages * PAGE_SIZE

    cs_p, cc_p, ss_p = _rope_tables(jnp.arange(seq, dtype=jnp.int32))
    cs_s, cc_s, ss_s = (jnp.broadcast_to(t, (n_db, LANES))
                        for t in _rope_tables(jnp.full((1,), past_len, jnp.int32)))

    order = np.asarray(SWA_HEAD_ORDER)
    slope_col = jnp.asarray([[_alibi_slope(h)] for h in SWA_HEAD_ORDER], F32)
    lane_kv = np.arange(SWA_KV)[None, :] // SWA_HEAD_DIM
    mask = jnp.asarray(lane_kv == (order // SWA_GROUP)[:, None], F32)
    sink_cols = swa_sinks[:, order][:, :, None]
    cache_kpe_t = jnp.swapaxes(cache_kpe, 2, 3)
    buf_k = state_swa_k.reshape(state_swa_k.shape[:3] + (SWA_KV,))
    buf_v = state_swa_v.reshape(state_swa_v.shape[:3] + (SWA_KV,))

    pre = pre_norm[:, None, :]
    post = post_norm[:, None, :]
    q_gain = mla_q_norm[:, None, :]
    kv_gain = mla_kv_norm[:, None, :]
    m_w_in, wqn, wqr, wqrot, wuk, wuvt, wukt, wuv3, m_w_o = _mla_weights(
        mla_w_in, mla_w_qb, mla_w_uk, mla_w_uv, mla_w_o)
    s_w_in, s_w_o = _swa_weights(swa_w_in, swa_w_o)

    hp = x_prompt.reshape(n_b * seq, D_MODEL)
    hs = x_sample.reshape(n_db, D_MODEL)
    to3 = lambda a: a.reshape(n_b, seq, a.shape[-1])
    ckv_p, kpe_p, ckv_s, kpe_s = [], [], [], []
    swk_p, swv_p, swk_s, swv_s = [], [], [], []
    for i in range(depth):
        l = i // 2
        if i % 2 == 0:
            cq, ckv, ckvb, kpe2, kpeb, sg = _mla_in(hp, pre, m_w_in, q_gain, kv_gain, cs_p,
                                                    ROW_TILE, i, l)
            qn, qr2, kn, vt = _mla_qkv(cq, ckvb, wqn, wqr, wqrot, wuk, wuvt, cc_p, ss_p,
                                       QKV_ROW_TILE, l)
            og = _mla_flash(to3(qn), to3(qr2), to3(kn), to3(kpeb), vt, to3(sg),
                            FLASH_Q_BLOCK, FLASH_KEY_CHUNK, FLASH_HEADS)
            hp = _out_proj(og.reshape(n_b * seq, MLA_WIDTH), m_w_o, post, hp, ROW_TILE, l, i)
            ckv_p.append(ckv.reshape(n_b, seq, KV_LORA))
            kpe_p.append(kpe2[:, :QK_ROPE].reshape(n_b, seq, QK_ROPE))
            cq, ckv, _, kpe2, _, sg = _mla_in(hs, pre, m_w_in, q_gain, kv_gain, cs_s, n_db, i, l)
            qn, qr = _mla_q(cq, wqn, wqr, wqrot, cc_s, ss_s, l)
            qlat = _qlat(qn, wukt, l)
            kpe_new = kpe2[:, :QK_ROPE]
            olat = _paged_attention(
                page_table, qlat.reshape(n_db, MLA_HEADS, KV_LORA),
                qr.reshape(n_db, MLA_HEADS, QK_ROPE), ckv.reshape(n_db, 1, KV_LORA),
                kpe_new.reshape(n_db, 1, QK_ROPE), cache_ckv, cache_kpe_t, l, PAGED_CHUNK_PAGES)
            og = _olat_proj(olat.reshape(n_db, MLA_HEADS * KV_LORA), wuv3, sg, l)
            hs = _out_proj(og, m_w_o, post, hs, n_db, l, i)
            ckv_s.append(ckv.reshape(n_db, 1, KV_LORA))
            kpe_s.append(kpe_new.reshape(n_db, 1, QK_ROPE))
        else:
            q, k, v, sg = _swa_in(hp, pre, s_w_in, ROW_TILE, i, l)
            og = _swa_prompt(swa_sinks, to3(q), to3(k), to3(v), to3(sg), l)
            hp = _out_proj(og.reshape(n_b * seq, SWA_Q), s_w_o, post, hp, ROW_TILE, l, i)
            keep = min(WINDOW, seq)
            tail = lambda a: to3(a)[:, seq - keep:].reshape(n_b, keep, SWA_KV_HEADS, SWA_HEAD_DIM)
            swk_p.append(tail(k))
            swv_p.append(tail(v))
            q, k, v, sg = _swa_in(hs, pre, s_w_in, n_db, i, l)
            nk, nv, og3 = _swa_sample(
                q.reshape(n_db, SWA_HEADS, SWA_HEAD_DIM), buf_k, buf_v,
                k.reshape(n_db, 1, SWA_KV), v.reshape(n_db, 1, SWA_KV),
                sg.reshape(n_db, SWA_HEADS, SWA_HEAD_DIM),
                sink_cols, slope_col, mask, l, SWA_SAMPLE_SEQS)
            hs = _out_proj(og3.reshape(n_db, SWA_Q), s_w_o, post, hs, n_db, l, i)
            swk_s.append(nk.reshape(n_db, WINDOW, SWA_KV_HEADS, SWA_HEAD_DIM))
            swv_s.append(nv.reshape(n_db, WINDOW, SWA_KV_HEADS, SWA_HEAD_DIM))
    return (hp.reshape(n_b, seq, D_MODEL), hs.reshape(n_db, t_new, D_MODEL),
            jnp.stack(ckv_p), jnp.stack(kpe_p), jnp.stack(ckv_s), jnp.stack(kpe_s),
            jnp.stack(swk_p), jnp.stack(swv_p), jnp.stack(swk_s), jnp.stack(swv_s))
```

```python
import functools
import math

import jax
import jax.numpy as jnp
import numpy as np
from jax import lax
from jax.experimental import pallas as pl
from jax.experimental.pallas import tpu as pltpu

F32 = jnp.float32
BF16 = jnp.bfloat16

D_MODEL = 1024
NORM_EPS = 1e-6
NEG_INF = -1e30
PAGE_SIZE = 128

MLA_HEADS = 16
Q_LORA = 512
KV_LORA = 256
QK_NOPE = 128
QK_ROPE = 64
V_HEAD = 128
ROPE_THETA = 10000.0
MLA_WIDTH = MLA_HEADS * V_HEAD
MLA_SCALE = 1.0 / math.sqrt(QK_NOPE + QK_ROPE)
MLA_QSCALE = MLA_SCALE * math.log2(math.e)
Q_ROWS = QK_NOPE + 128

SWA_HEADS = 16
SWA_KV_HEADS = 4
SWA_GROUP = SWA_HEADS // SWA_KV_HEADS
SWA_HEAD_DIM = 64
WINDOW = 128
SWA_Q = SWA_HEADS * SWA_HEAD_DIM
SWA_KV = SWA_KV_HEADS * SWA_HEAD_DIM
SWA_SCALE = 1.0 / math.sqrt(SWA_HEAD_DIM)

LANES = 128
VMEM_LIMIT = 56 * 1024 * 1024

ROW_TILE = 512
QKV_ROW_TILE = 256
FLASH_Q_BLOCK = 512
FLASH_KEY_CHUNK = 256
FLASH_HEADS = 4
PAGED_CHUNK_PAGES = 16
SWA_SAMPLE_SEQS = 8

SWA_HEAD_ORDER = tuple(
    (2 * i + par) * SWA_GROUP + g
    for i in range(SWA_KV_HEADS // 2) for g in range(SWA_GROUP) for par in range(2))

NT_DIMS = (((1,), (1,)), ((), ()))


def _params(*sem):
    return pltpu.CompilerParams(dimension_semantics=sem, vmem_limit_bytes=VMEM_LIMIT)


def _rms(x, g):
    return x * lax.rsqrt(jnp.mean(x * x, axis=-1, keepdims=True) + NORM_EPS) * g


def _silu(x):
    return x * (1.0 / (1.0 + jnp.exp(-x)))


def _full(shape):
    return pl.BlockSpec(shape, lambda *_: (0,) * len(shape))


def _of_layer(stacked, layer):
    shape = stacked.shape[1:]
    return pl.BlockSpec((None,) + shape, lambda *_: (layer,) + (0,) * len(shape))


def _mla_in_kernel(h_ref, g_ref, w_ref, qn_ref, kvn_ref, cs_ref,
                   cq_ref, ckv_ref, ckvb_ref, kpe_ref, kpeb_ref, sg_ref):
    a = _rms(h_ref[...], g_ref[...]).astype(BF16)
    z = jnp.dot(a, w_ref[...], preferred_element_type=F32)
    cq_ref[...] = _rms(z[:, :Q_LORA], qn_ref[...]).astype(BF16)
    ckv = _rms(z[:, Q_LORA:Q_LORA + KV_LORA], kvn_ref[...])
    ckv_ref[...] = ckv
    ckvb_ref[...] = ckv.astype(BF16)
    g0 = Q_LORA + KV_LORA
    sg_ref[...] = _silu(z[:, g0:g0 + MLA_WIDTH])
    t = z[:, g0 + MLA_WIDTH:] * cs_ref[...]
    kpe2 = t + pltpu.roll(t, QK_ROPE, axis=1)
    kpe_ref[...] = kpe2
    kpeb_ref[...] = kpe2.astype(BF16)


def _mla_in(h, gains, w, qn, kvn, cs, tm, depth_i, layer):
    rows = h.shape[0]
    n_cs = cs.shape[0] // tm
    row = lambda r: (r, 0)
    outs = (
        jax.ShapeDtypeStruct((rows, Q_LORA), BF16),
        jax.ShapeDtypeStruct((rows, KV_LORA), F32),
        jax.ShapeDtypeStruct((rows, KV_LORA), BF16),
        jax.ShapeDtypeStruct((rows, LANES), F32),
        jax.ShapeDtypeStruct((rows, LANES), BF16),
        jax.ShapeDtypeStruct((rows, MLA_WIDTH), F32),
    )
    return pl.pallas_call(
        _mla_in_kernel,
        grid=(rows // tm,),
        in_specs=[
            pl.BlockSpec((tm, D_MODEL), row),
            _of_layer(gains, depth_i), _of_layer(w, layer),
            _of_layer(qn, layer), _of_layer(kvn, layer),
            pl.BlockSpec((tm, LANES), lambda r: (r % n_cs, 0)),
        ],
        out_specs=tuple(pl.BlockSpec((tm, o.shape[1]), row) for o in outs),
        out_shape=outs,
        compiler_params=_params("parallel"),
        name="mla_in",
    )(h, gains, w, qn, kvn, cs)


def _rope_groups(cq, wr_ref, wrot_ref, cc, ss):
    r = jnp.dot(cq, wr_ref[...], preferred_element_type=F32)
    rr = jnp.dot(cq, wrot_ref[...], preferred_element_type=F32)
    n_grp = r.shape[1] // LANES
    return [r[:, j * LANES:(j + 1) * LANES] * cc + rr[:, j * LANES:(j + 1) * LANES] * ss
            for j in range(n_grp)]


def _mla_qkv_kernel(cq_ref, ckvb_ref, wqnt_ref, wqrt_ref, wqrott_ref, wuk_ref, wuvt_ref,
                    cct_ref, sst_ref, qt_ref, kn_ref, vt_ref):
    cq = cq_ref[...]
    qnt = lax.dot_general(wqnt_ref[...], cq, NT_DIMS, preferred_element_type=F32)
    rt = lax.dot_general(wqrt_ref[...], cq, NT_DIMS, preferred_element_type=F32)
    rrt = lax.dot_general(wqrott_ref[...], cq, NT_DIMS, preferred_element_type=F32)
    cct, sst = cct_ref[...], sst_ref[...]
    low = lax.broadcasted_iota(jnp.int32, cct.shape, 0) < QK_ROPE
    for j in range(MLA_HEADS // 2):
        rows = slice(j * LANES, (j + 1) * LANES)
        grp = (rt[rows, :] * cct + rrt[rows, :] * sst) * MLA_QSCALE
        for par, rope in enumerate((jnp.where(low, grp, 0.0), jnp.where(low, 0.0, grp))):
            h = 2 * j + par
            nope = qnt[h * QK_NOPE:(h + 1) * QK_NOPE, :] * MLA_QSCALE
            qt_ref[h * Q_ROWS:h * Q_ROWS + QK_NOPE, :] = nope.astype(BF16)
            qt_ref[h * Q_ROWS + QK_NOPE:(h + 1) * Q_ROWS, :] = rope.astype(BF16)
    ckv = ckvb_ref[...]
    kn_ref[...] = jnp.dot(ckv, wuk_ref[...], preferred_element_type=F32).astype(BF16)
    vt_ref[...] = lax.dot_general(wuvt_ref[...], ckv, NT_DIMS,
                                  preferred_element_type=F32).astype(BF16)


def _mla_qkv(cq, ckvb, wqnt, wqrt, wqrott, wuk, wuvt, cct, sst, tm, layer):
    rows = cq.shape[0]
    n_cs = cct.shape[1] // tm
    row = lambda r: (r, 0)
    col = lambda r: (0, r)
    tab = pl.BlockSpec((LANES, tm), lambda r: (0, r % n_cs))
    return pl.pallas_call(
        _mla_qkv_kernel,
        grid=(rows // tm,),
        in_specs=[
            pl.BlockSpec((tm, Q_LORA), row),
            pl.BlockSpec((tm, KV_LORA), row),
            _of_layer(wqnt, layer), _of_layer(wqrt, layer), _of_layer(wqrott, layer),
            _of_layer(wuk, layer), _of_layer(wuvt, layer), tab, tab,
        ],
        out_specs=(pl.BlockSpec((MLA_HEADS * Q_ROWS, tm), col),
                   pl.BlockSpec((tm, MLA_WIDTH), row),
                   pl.BlockSpec((MLA_WIDTH, tm), col)),
        out_shape=(jax.ShapeDtypeStruct((MLA_HEADS * Q_ROWS, rows), BF16),
                   jax.ShapeDtypeStruct((rows, MLA_WIDTH), BF16),
                   jax.ShapeDtypeStruct((MLA_WIDTH, rows), BF16)),
        compiler_params=_params("parallel"),
        name="mla_qkv",
    )(cq, ckvb, wqnt, wqrt, wqrott, wuk, wuvt, cct, sst)


def _mla_q_kernel(cq_ref, wqn_ref, wqr_ref, wqrot_ref, cc_ref, ss_ref, qn_ref, qr_ref):
    cq = cq_ref[...]
    qn_ref[...] = jnp.dot(cq, wqn_ref[...], preferred_element_type=F32).astype(BF16)
    groups = _rope_groups(cq, wqr_ref, wqrot_ref, cc_ref[...], ss_ref[...])
    for j, grp in enumerate(groups):
        qr_ref[:, j * LANES:(j + 1) * LANES] = grp.astype(BF16)


def _mla_q(cq, wqn, wqr, wqrot, cc, ss, layer):
    rows = cq.shape[0]
    return pl.pallas_call(
        _mla_q_kernel,
        grid=(1,),
        in_specs=[_full(cq.shape), _of_layer(wqn, layer), _of_layer(wqr, layer),
                  _of_layer(wqrot, layer), _full(cc.shape), _full(ss.shape)],
        out_specs=(_full((rows, MLA_WIDTH)), _full((rows, MLA_HEADS * QK_ROPE))),
        out_shape=(jax.ShapeDtypeStruct((rows, MLA_WIDTH), BF16),
                   jax.ShapeDtypeStruct((rows, MLA_HEADS * QK_ROPE), BF16)),
        compiler_params=_params("arbitrary"),
        name="mla_q_sample",
    )(cq, wqn, wqr, wqrot, cc, ss)


def _mla_flash_kernel(qt_ref, kn_ref, kpe_ref, vt_ref, sg_ref, o_ref, s_scr, mx_scr,
                      acc_scr, *, blk, sub, heads):
    i = pl.program_id(2)
    lanes = [slice(h * LANES, (h + 1) * LANES) for h in range(heads)]
    qt = [qt_ref[h * Q_ROWS:(h + 1) * Q_ROWS, :] for h in range(heads)]

    def scores(h, start, col0, slot, diagonal):
        rows = pl.ds(start, sub)
        k = jnp.concatenate([kn_ref[rows, lanes[h]], kpe_ref[rows, :]], axis=-1)
        s = jnp.dot(k, qt[h][:, col0:], preferred_element_type=F32)
        if diagonal:
            s = causal(s)
        s_scr[h, slot, :, col0:] = s
        mx_scr[h, slot, :, col0:] = jnp.max(s, axis=0, keepdims=True)

    def causal(s):
        ki = lax.broadcasted_iota(jnp.int32, s.shape, 0)
        qi = lax.broadcasted_iota(jnp.int32, s.shape, 1)
        return jnp.where(ki <= qi, s, NEG_INF)

    def fold(h, slot, start, col0, carry, mask_now):
        m, l = (c[:, col0:] for c in carry)
        rows = pl.ds(start, sub)
        s = s_scr[h, slot, :, col0:]
        if mask_now:
            s = causal(s)
            m_new = jnp.maximum(m, jnp.max(s, axis=0, keepdims=True))
        else:
            m_new = jnp.maximum(m, mx_scr[h, slot, :, col0:])
        alpha = jnp.exp2(m - m_new)
        p = jnp.exp2(s - m_new)
        l = alpha * l + jnp.sum(p, axis=0, keepdims=True)
        acc_scr[h, :, col0:] = alpha * acc_scr[h, :, col0:] + jnp.dot(
            vt_ref[lanes[h], rows], p.astype(BF16), preferred_element_type=F32)
        if col0 == 0:
            return m_new, l
        return tuple(jnp.concatenate([c[:, :col0], n], axis=1)
                     for c, n in zip(carry, (m_new, l)))

    n_sub = blk // sub

    def full_block(j, carries):
        base = pl.multiple_of(j * blk, blk)
        carries = list(carries)
        for u in range(n_sub):
            for h in range(heads):
                scores(h, base + (u + 1) * sub, 0, (u + 1) % 2, False)
            for h in range(heads):
                carries[h] = fold(h, u % 2, base + u * sub, 0, carries[h], False)
        return tuple(carries)

    init = (jnp.full((1, blk), NEG_INF, F32), jnp.zeros((1, blk), F32))
    acc_scr[...] = jnp.zeros(acc_scr.shape, F32)
    for h in range(heads):
        scores(h, 0, 0, 0, False)
    carries = list(lax.fori_loop(0, i, full_block, (init,) * heads))
    base = pl.multiple_of(i * blk, blk)
    for u in range(n_sub):
        if u + 1 < n_sub:
            for h in range(heads):
                scores(h, base + (u + 1) * sub, (u + 1) * sub, (u + 1) % 2, True)
        for h in range(heads):
            carries[h] = fold(h, u % 2, base + u * sub, u * sub, carries[h], u == 0)
    for h in range(heads):
        _, l = carries[h]
        o_ref[:, lanes[h]] = ((acc_scr[h] / l).T * sg_ref[:, lanes[h]]).astype(BF16)


def _mla_flash(qt, kn, kpeb, vt, sg, blk, sub, heads):
    b, s, _ = kn.shape
    width = heads * LANES
    n_q = s // blk
    q_spec = pl.BlockSpec((None, blk, width), lambda bi, h, i: (bi, i, h))
    return pl.pallas_call(
        functools.partial(_mla_flash_kernel, blk=blk, sub=sub, heads=heads),
        grid=(b, MLA_HEADS // heads, n_q),
        in_specs=[pl.BlockSpec((heads * Q_ROWS, blk), lambda bi, h, i: (h, bi * n_q + i)),
                  pl.BlockSpec((None, s, width), lambda bi, h, i: (bi, 0, h)),
                  pl.BlockSpec((None, s, LANES), lambda bi, h, i: (bi, 0, 0)),
                  pl.BlockSpec((heads * V_HEAD, s), lambda bi, h, i: (h, bi)),
                  q_spec],
        out_specs=q_spec,
        out_shape=jax.ShapeDtypeStruct((b, s, MLA_WIDTH), BF16),
        scratch_shapes=[pltpu.VMEM((heads, 2, sub, blk), F32),
                        pltpu.VMEM((heads, 2, 1, blk), F32),
                        pltpu.VMEM((heads, V_HEAD, blk), F32)],
        compiler_params=_params("parallel", "parallel", "arbitrary"),
        name="mla_flash",
    )(qt, kn, kpeb, vt, sg)


def _out_kernel(x_ref, w_ref, g_ref, h_ref, o_ref):
    y = jnp.dot(x_ref[...], w_ref[...], preferred_element_type=F32)
    o_ref[...] = h_ref[...] + _rms(y, g_ref[...])


def _out_proj(x, w, gains, h, tm, layer, depth_i):
    rows, width = x.shape
    row = lambda r: (r, 0)
    return pl.pallas_call(
        _out_kernel,
        grid=(rows // tm,),
        in_specs=[pl.BlockSpec((tm, width), row), _of_layer(w, layer), _of_layer(gains, depth_i),
                  pl.BlockSpec((tm, D_MODEL), row)],
        out_specs=pl.BlockSpec((tm, D_MODEL), row),
        out_shape=jax.ShapeDtypeStruct((rows, D_MODEL), F32),
        compiler_params=_params("parallel"),
        name="out_proj",
    )(x, w, gains, h)


def _qlat_kernel(qn_ref, wukt_ref, o_ref):
    o_ref[...] = jnp.dot(qn_ref[...], wukt_ref[...], preferred_element_type=F32).astype(BF16)


def _qlat(qn, wukt, layer):
    rows = qn.shape[0]
    return pl.pallas_call(
        _qlat_kernel,
        grid=(MLA_HEADS,),
        in_specs=[pl.BlockSpec((rows, QK_NOPE), lambda h: (0, h)),
                  pl.BlockSpec((None, None, QK_NOPE, KV_LORA), lambda h: (layer, h, 0, 0))],
        out_specs=pl.BlockSpec((rows, KV_LORA), lambda h: (0, h)),
        out_shape=jax.ShapeDtypeStruct((rows, MLA_HEADS * KV_LORA), BF16),
        compiler_params=_params("parallel"),
        name="mla_qlat",
    )(qn, wukt)


def _paged_kernel(pt_ref, ql_ref, qr_ref, cnew_ref, knew_ref, ckv_hbm, kpe_hbm, o_ref,
                  ckv0, ckv1, kpe0, kpe1, kb, kbt, pb, s_scr, e_scr, sem, *, layer, n_pages,
                  chunk):
    g = pl.program_id(0)
    n_seq = 2 * pl.num_programs(0)
    bufs = ((ckv0, kpe0), (ckv1, kpe1))

    def page_copies(seq, sl, p):
        pid = pt_ref[seq, p]
        ckv_buf, kpe_buf = bufs[sl]
        return (pltpu.make_async_copy(ckv_hbm.at[layer, pid],
                                      ckv_buf.at[pl.ds(p * PAGE_SIZE, PAGE_SIZE), :],
                                      sem.at[0, sl]),
                pltpu.make_async_copy(kpe_hbm.at[layer, pid], kpe_buf.at[p], sem.at[1, sl]))

    def start_page(seq, sl, p):
        for i, copy in enumerate(page_copies(seq, sl, p)):
            copy.start(priority=(p + i) % 2)

    def wait_gather(seq, sl):
        for p in range(n_pages):
            for copy in page_copies(seq, sl, p):
                copy.wait()

    def chunk_keys(c):
        return slice(c * chunk * PAGE_SIZE, (c + 1) * chunk * PAGE_SIZE)

    def score_chunk(sl, c, nxt):
        ckv_buf, kpe_buf = bufs[sl]
        for p in range(c * chunk, (c + 1) * chunk):
            start_page(nxt, 1 - sl, p)
            keys = slice(p * PAGE_SIZE, (p + 1) * PAGE_SIZE)
            page = ckv_buf[keys, :]
            page_b = page.astype(BF16)
            kb[sl, keys, :] = page_b
            kbt[:, keys] = page_b.T
            pb[:, keys] = kpe_buf[p].astype(BF16)
        keys = chunk_keys(c)
        s_scr[:, keys] = (
            jnp.dot(ql_ref[sl], kbt[:, keys], preferred_element_type=F32)
            + jnp.dot(qr_ref[sl], pb[:, keys], preferred_element_type=F32)) * MLA_SCALE

    def softmax(sl):
        s = s_scr[...]
        cn = cnew_ref[sl].astype(BF16).astype(F32)
        kn = knew_ref[sl].astype(BF16).astype(F32)
        s_new = (jnp.sum(ql_ref[sl].astype(F32) * cn, axis=-1, keepdims=True)
                 + jnp.sum(qr_ref[sl].astype(F32) * kn, axis=-1, keepdims=True)) * MLA_SCALE
        m = jnp.maximum(jnp.max(s, axis=-1, keepdims=True), s_new)
        e = jnp.exp(s - m)
        e_new = jnp.exp(s_new - m)
        e_scr[sl] = e.astype(BF16)
        return e_new.astype(BF16).astype(F32) * cn, jnp.sum(e, axis=-1, keepdims=True) + e_new

    def values(sl, c):
        keys = chunk_keys(c)
        return jnp.dot(e_scr[sl, :, keys], kb[sl, keys, :], preferred_element_type=F32)

    @pl.when(g == 0)
    def _():
        for p in range(n_pages):
            start_page(0, 0, p)

    n_chunks = n_pages // chunk
    first = 2 * g
    after = jnp.minimum(first + 2, n_seq - 1)
    wait_gather(first, 0)
    for c in range(n_chunks):
        score_chunk(0, c, first + 1)
    acc, den = softmax(0)
    wait_gather(first + 1, 1)
    for c in range(n_chunks):
        score_chunk(1, c, after)
        acc = acc + values(0, c)
    o_ref[0] = acc / den
    acc, den = softmax(1)
    for c in range(n_chunks):
        acc = acc + values(1, c)
    o_ref[1] = acc / den

    @pl.when(g == pl.num_programs(0) - 1)
    def _():
        wait_gather(n_seq - 1, 0)


def _paged_attention(page_table, qlat3, qr3, ckv_new, kpe_new, cache_ckv, cache_kpe_t, layer,
                     chunk):
    nb, n_pages = page_table.shape
    n_keys = n_pages * PAGE_SIZE
    pair = lambda shape: pl.BlockSpec((2,) + shape, lambda g, pt: (g, 0, 0))
    hbm = pl.BlockSpec(memory_space=pl.ANY)
    grid_spec = pltpu.PrefetchScalarGridSpec(
        num_scalar_prefetch=1,
        grid=(nb // 2,),
        in_specs=[pair((MLA_HEADS, KV_LORA)), pair((MLA_HEADS, QK_ROPE)),
                  pair((1, KV_LORA)), pair((1, QK_ROPE)), hbm, hbm],
        out_specs=pair((MLA_HEADS, KV_LORA)),
        scratch_shapes=[
            pltpu.VMEM((n_keys, KV_LORA), F32),
            pltpu.VMEM((n_keys, KV_LORA), F32),
            pltpu.VMEM((n_pages, QK_ROPE, PAGE_SIZE), F32),
            pltpu.VMEM((n_pages, QK_ROPE, PAGE_SIZE), F32),
            pltpu.VMEM((2, n_keys, KV_LORA), BF16),
            pltpu.VMEM((KV_LORA, n_keys), BF16),
            pltpu.VMEM((QK_ROPE, n_keys), BF16),
            pltpu.VMEM((MLA_HEADS, n_keys), F32),
            pltpu.VMEM((2, MLA_HEADS, n_keys), BF16),
            pltpu.SemaphoreType.DMA((2, 2)),
        ],
    )
    return pl.pallas_call(
        functools.partial(_paged_kernel, layer=layer, n_pages=n_pages, chunk=chunk),
        grid_spec=grid_spec,
        out_shape=jax.ShapeDtypeStruct((nb, MLA_HEADS, KV_LORA), F32),
        compiler_params=_params("arbitrary"),
        name="mla_paged",
    )(page_table, qlat3, qr3, ckv_new, kpe_new, cache_ckv, cache_kpe_t)


def _olat_kernel(ol_ref, wuv_ref, sg_ref, o_ref):
    o = jnp.dot(ol_ref[...].astype(BF16), wuv_ref[...], preferred_element_type=F32)
    o_ref[...] = (o * sg_ref[...]).astype(BF16)


def _olat_proj(olat, wuv3, sg, layer):
    rows = olat.shape[0]
    return pl.pallas_call(
        _olat_kernel,
        grid=(MLA_HEADS,),
        in_specs=[pl.BlockSpec((rows, KV_LORA), lambda h: (0, h)),
                  pl.BlockSpec((None, None, KV_LORA, V_HEAD), lambda h: (layer, h, 0, 0)),
                  pl.BlockSpec((rows, V_HEAD), lambda h: (0, h))],
        out_specs=pl.BlockSpec((rows, V_HEAD), lambda h: (0, h)),
        out_shape=jax.ShapeDtypeStruct((rows, MLA_WIDTH), BF16),
        compiler_params=_params("parallel"),
        name="mla_olat",
    )(olat, wuv3, sg)


def _swa_in_kernel(h_ref, g_ref, w_ref, q_ref, k_ref, v_ref, sg_ref):
    a = _rms(h_ref[...], g_ref[...]).astype(BF16)
    z = jnp.dot(a, w_ref[...], preferred_element_type=F32)
    q_ref[...] = (z[:, :SWA_Q] * SWA_SCALE).astype(BF16)
    k_ref[...] = z[:, SWA_Q:SWA_Q + SWA_KV]
    v_ref[...] = z[:, SWA_Q + SWA_KV:SWA_Q + 2 * SWA_KV]
    sg_ref[...] = _silu(z[:, SWA_Q + 2 * SWA_KV:])


def _swa_in(h, gains, w, tm, depth_i, layer):
    rows = h.shape[0]
    row = lambda r: (r, 0)
    outs = (jax.ShapeDtypeStruct((rows, SWA_Q), BF16),
            jax.ShapeDtypeStruct((rows, SWA_KV), F32),
            jax.ShapeDtypeStruct((rows, SWA_KV), F32),
            jax.ShapeDtypeStruct((rows, SWA_Q), F32))
    return pl.pallas_call(
        _swa_in_kernel,
        grid=(rows // tm,),
        in_specs=[pl.BlockSpec((tm, D_MODEL), row), _of_layer(gains, depth_i),
                  _of_layer(w, layer)],
        out_specs=tuple(pl.BlockSpec((tm, o.shape[1]), row) for o in outs),
        out_shape=outs,
        compiler_params=_params("parallel"),
        name="swa_in",
    )(h, gains, w)


def _sink_softmax(s, sink):
    m = jnp.maximum(jnp.max(s, axis=-1, keepdims=True), sink)
    e = jnp.exp(s - m)
    return e / (jnp.sum(e, axis=-1, keepdims=True) + jnp.exp(sink - m))


def _alibi_slope(head):
    return 2.0 ** (-8.0 * (head + 1) / SWA_HEADS)


def _swa_bias():
    dist = (WINDOW + np.arange(WINDOW))[None, :] - np.arange(2 * WINDOW)[:, None]
    in_window = (dist >= 0) & (dist < WINDOW)
    slopes = np.asarray([_alibi_slope(h) for h in range(SWA_HEADS)], np.float32)
    bias = -slopes[:, None, None] * dist[None].astype(np.float32)
    return np.where(in_window[None], bias, np.float32(NEG_INF)).astype(np.float32)


def _swa_prompt_kernel(sink_ref, q_ref, kp_ref, kc_ref, vp_ref, vc_ref, sg_ref, bias_ref, o_ref,
                       *, layer):
    n = pl.program_id(1)
    kk = jnp.concatenate([kp_ref[...], kc_ref[...]], axis=0).astype(BF16)
    vvt = jnp.concatenate([vp_ref[...], vc_ref[...]], axis=0).T.astype(BF16)
    k_low = lax.broadcasted_iota(jnp.int32, (2 * WINDOW, LANES), 1) < SWA_HEAD_DIM
    v_low = lax.broadcasted_iota(jnp.int32, (LANES, 2 * WINDOW), 0) < SWA_HEAD_DIM
    zero = jnp.zeros((), BF16)
    prev_pen = jnp.where(n > 0, 0.0, NEG_INF).astype(F32)
    span = lambda g: slice(g * WINDOW, (g + 1) * WINDOW)
    pairs = range(SWA_KV_HEADS // 2)
    kv_span = [slice(pair * LANES, (pair + 1) * LANES) for pair in pairs]
    grp_lanes = [[slice((pair * SWA_GROUP + g) * LANES, (pair * SWA_GROUP + g + 1) * LANES)
                  for g in range(SWA_GROUP)] for pair in pairs]
    st_all = {}
    for pair in pairs:
        q_st = jnp.concatenate([q_ref[:, lanes] for lanes in grp_lanes[pair]], axis=0)
        for par in range(2):
            keep_k = k_low if par == 0 else ~k_low
            st_all[pair, par] = lax.dot_general(
                jnp.where(keep_k, kk[:, kv_span[pair]], zero), q_st, NT_DIMS,
                preferred_element_type=F32)
    for pair in pairs:
        vt_g = vvt[kv_span[pair], :]
        ot_pair = jnp.zeros((LANES, SWA_GROUP * WINDOW), F32)
        for par in range(2):
            keep_v = v_low if par == 0 else ~v_low
            es, inv_dens = [], []
            for g in range(SWA_GROUP):
                head = SWA_HEAD_ORDER[2 * (pair * SWA_GROUP + g) + par]
                s = st_all[pair, par][:, span(g)] + bias_ref[head]
                s_prev = s[:WINDOW, :] + prev_pen
                s_cur = s[WINDOW:, :]
                sink = sink_ref[layer, head]
                m = jnp.maximum(jnp.maximum(jnp.max(s_prev, axis=0, keepdims=True),
                                            jnp.max(s_cur, axis=0, keepdims=True)), sink)
                e_prev = jnp.exp(s_prev - m)
                e_cur = jnp.exp(s_cur - m)
                den = (jnp.sum(e_prev, axis=0, keepdims=True)
                       + jnp.sum(e_cur, axis=0, keepdims=True) + jnp.exp(sink - m))
                es.append(jnp.concatenate([e_prev, e_cur], axis=0).astype(BF16))
                inv_dens.append(1.0 / den)
            ot_par = jnp.dot(jnp.where(keep_v, vt_g, zero), jnp.concatenate(es, axis=1),
                             preferred_element_type=F32)
            ot_pair = ot_pair + ot_par * jnp.concatenate(inv_dens, axis=1)
        o_pair = ot_pair.T
        for g, lanes in enumerate(grp_lanes[pair]):
            o_ref[:, lanes] = (o_pair[span(g), :] * sg_ref[:, lanes]).astype(BF16)


def _swa_prompt(sinks, q, k, v, sg, layer):
    b, s, _ = q.shape
    cur = lambda bi, n: (bi, n, 0)
    prev = lambda bi, n: (bi, jnp.maximum(n - 1, 0), 0)
    wide = lambda idx: pl.BlockSpec((None, WINDOW, SWA_Q), idx)
    narrow = lambda idx: pl.BlockSpec((None, WINDOW, SWA_KV), idx)
    bias = _swa_bias()
    return pl.pallas_call(
        functools.partial(_swa_prompt_kernel, layer=layer),
        grid=(b, s // WINDOW),
        in_specs=[pl.BlockSpec(memory_space=pltpu.SMEM),
                  wide(cur), narrow(prev), narrow(cur), narrow(prev), narrow(cur), wide(cur),
                  _full(bias.shape)],
        out_specs=wide(cur),
        out_shape=jax.ShapeDtypeStruct((b, s, SWA_Q), BF16),
        compiler_params=_params("parallel", "parallel"),
        name="swa_prompt",
    )(sinks, q, k, k, v, v, sg, jnp.asarray(bias))


def _swa_sample_kernel(q_ref, bk_ref, bv_ref, kn_ref, vn_ref, sg_ref, sink_ref, slope_ref,
                       mask_ref, nk_ref, nv_ref, o_ref):
    n_seq = q_ref.shape[0]
    mask = mask_ref[...] > 0.5
    scores = []
    for j in range(n_seq):
        nk_ref[j, 0:WINDOW - 1, :] = bk_ref[j, 1:WINDOW, :]
        nk_ref[j, WINDOW - 1:WINDOW, :] = kn_ref[j]
        nv_ref[j, 0:WINDOW - 1, :] = bv_ref[j, 1:WINDOW, :]
        nv_ref[j, WINDOW - 1:WINDOW, :] = vn_ref[j]
        q_exp = jnp.where(mask, jnp.concatenate([q_ref[j]] * SWA_KV_HEADS, axis=-1),
                          jnp.zeros((), BF16))
        scores.append(lax.dot_general(q_exp, nk_ref[j].astype(BF16), NT_DIMS,
                                      preferred_element_type=F32))
    dist = (WINDOW - 1 - lax.broadcasted_iota(jnp.int32, scores[0].shape, 1)).astype(F32)
    bias = slope_ref[...] * dist
    probs = [_sink_softmax(s - bias, sink_ref[...]).astype(BF16) for s in scores]
    for j in range(n_seq):
        o_all = jnp.where(mask, jnp.dot(probs[j], nv_ref[j].astype(BF16),
                                        preferred_element_type=F32), 0.0)
        o = o_all[:, 0:SWA_HEAD_DIM]
        for kh in range(1, SWA_KV_HEADS):
            o = o + o_all[:, kh * SWA_HEAD_DIM:(kh + 1) * SWA_HEAD_DIM]
        o_ref[j] = (o * sg_ref[j]).astype(BF16)


def _swa_sample(q3, buf_k, buf_v, k_new, v_new, sg3, sink_col, slope_col, mask, layer, per_step):
    nb = q3.shape[0]
    per_b = lambda shape: pl.BlockSpec((per_step,) + shape, lambda b: (b, 0, 0))
    buf = pl.BlockSpec((None, per_step, WINDOW, SWA_KV), lambda b: (layer, b, 0, 0))
    return pl.pallas_call(
        _swa_sample_kernel,
        grid=(nb // per_step,),
        in_specs=[per_b((SWA_HEADS, SWA_HEAD_DIM)), buf, buf,
                  per_b((1, SWA_KV)), per_b((1, SWA_KV)), per_b((SWA_HEADS, SWA_HEAD_DIM)),
                  _of_layer(sink_col, layer), _full((SWA_HEADS, 1)),
                  _full((SWA_HEADS, SWA_KV))],
        out_specs=(per_b((WINDOW, SWA_KV)), per_b((WINDOW, SWA_KV)),
                   per_b((SWA_HEADS, SWA_HEAD_DIM))),
        out_shape=(jax.ShapeDtypeStruct((nb, WINDOW, SWA_KV), F32),
                   jax.ShapeDtypeStruct((nb, WINDOW, SWA_KV), F32),
                   jax.ShapeDtypeStruct((nb, SWA_HEADS, SWA_HEAD_DIM), BF16)),
        compiler_params=_params("parallel"),
        name="swa_sample",
    )(q3, buf_k, buf_v, k_new, v_new, sg3, sink_col, slope_col, mask)


def _rot_cols(w):
    half = QK_ROPE // 2
    return jnp.concatenate([-w[..., half:], w[..., :half]], axis=-1)


def _rope_tables(pos):
    inv = ROPE_THETA ** (-jnp.arange(0, QK_ROPE, 2, dtype=F32) / QK_ROPE)
    ang = pos.astype(F32)[:, None] * inv[None, :]
    ang = jnp.concatenate([ang, ang], axis=-1)
    cos, sin = jnp.cos(ang), jnp.sin(ang)
    return (jnp.concatenate([cos, sin], axis=-1), jnp.concatenate([cos, cos], axis=-1),
            jnp.concatenate([sin, sin], axis=-1))


def _mla_weights(w_in, w_qb, w_uk, w_uv, w_o):
    n_l = w_in.shape[0]
    g0 = Q_LORA + KV_LORA
    w_kpe = w_in[..., g0:g0 + QK_ROPE]
    w_in_r = jnp.concatenate(
        [w_in[..., :g0], w_in[..., g0 + QK_ROPE:], w_kpe, _rot_cols(w_kpe)], axis=-1).astype(BF16)
    wqn = w_qb[..., :QK_NOPE].reshape(n_l, Q_LORA, MLA_HEADS * QK_NOPE).astype(BF16)
    wqr3 = w_qb[..., QK_NOPE:]
    wqr = wqr3.reshape(n_l, Q_LORA, MLA_HEADS * QK_ROPE).astype(BF16)
    wqrot = _rot_cols(wqr3).reshape(n_l, Q_LORA, MLA_HEADS * QK_ROPE).astype(BF16)
    wuk = w_uk.reshape(n_l, KV_LORA, MLA_HEADS * QK_NOPE).astype(BF16)
    wuvt = jnp.swapaxes(w_uv.reshape(n_l, KV_LORA, MLA_HEADS * V_HEAD), 1, 2).astype(BF16)
    wukt = jnp.transpose(w_uk, (0, 2, 3, 1)).astype(BF16)
    wuv3 = jnp.transpose(w_uv, (0, 2, 1, 3)).astype(BF16)
    return w_in_r, wqn, wqr, wqrot, wuk, wuvt, wukt, wuv3, w_o.astype(BF16)


def _swa_weights(w_in, w_o):
    cols = np.concatenate([np.arange(h * SWA_HEAD_DIM, (h + 1) * SWA_HEAD_DIM)
                           for h in SWA_HEAD_ORDER])
    g0 = SWA_Q + 2 * SWA_KV
    w_in_r = jnp.concatenate(
        [w_in[..., :SWA_Q][..., cols], w_in[..., SWA_Q:g0], w_in[..., g0:][..., cols]],
        axis=-1).astype(BF16)
    return w_in_r, w_o[:, cols, :].astype(BF16)


def kernel(x_prompt, x_sample, cache_ckv, cache_kpe, state_swa_k, state_swa_v, page_table,
           pre_norm, post_norm, mla_w_in, mla_q_norm, mla_w_qb, mla_kv_norm, mla_w_uk,
           mla_w_uv, mla_w_o, swa_w_in, swa_sinks, swa_w_o):
    n_b, seq, _ = x_prompt.shape
    n_db, t_new, _ = x_sample.shape
    n_pages = page_table.shape[1]
    assert t_new == 1 and state_swa_k.shape[2] == WINDOW
    assert seq % ROW_TILE == 0 and seq % FLASH_Q_BLOCK == 0
    assert n_db % 2 == 0 and n_db % SWA_SAMPLE_SEQS == 0 and n_pages % PAGED_CHUNK_PAGES == 0
    depth = pre_norm.shape[0]
    past_len = n_pages * PAGE_SIZE

    cs_p, cc_p, ss_p = _rope_tables(jnp.arange(seq, dtype=jnp.int32))
    cs_s, cc_s, ss_s = (jnp.broadcast_to(t, (n_db, LANES))
                        for t in _rope_tables(jnp.full((1,), past_len, jnp.int32)))

    order = np.asarray(SWA_HEAD_ORDER)
    slope_col = jnp.asarray([[_alibi_slope(h)] for h in SWA_HEAD_ORDER], F32)
    lane_kv = np.arange(SWA_KV)[None, :] // SWA_HEAD_DIM
    mask = jnp.asarray(lane_kv == (order // SWA_GROUP)[:, None], F32)
    sink_cols = swa_sinks[:, order][:, :, None]
    cache_kpe_t = jnp.swapaxes(cache_kpe, 2, 3)
    buf_k = state_swa_k.reshape(state_swa_k.shape[:3] + (SWA_KV,))
    buf_v = state_swa_v.reshape(state_swa_v.shape[:3] + (SWA_KV,))

    pre = pre_norm[:, None, :]
    post = post_norm[:, None, :]
    q_gain = mla_q_norm[:, None, :]
    kv_gain = mla_kv_norm[:, None, :]
    m_w_in, wqn, wqr, wqrot, wuk, wuvt, wukt, wuv3, m_w_o = _mla_weights(
        mla_w_in, mla_w_qb, mla_w_uk, mla_w_uv, mla_w_o)
    s_w_in, s_w_o = _swa_weights(swa_w_in, swa_w_o)
    wqnt, wqrt, wqrott = (jnp.swapaxes(w, 1, 2) for w in (wqn, wqr, wqrot))
    cct_p, sst_p = cc_p.T, ss_p.T

    hp = x_prompt.reshape(n_b * seq, D_MODEL)
    hs = x_sample.reshape(n_db, D_MODEL)
    to3 = lambda a: a.reshape(n_b, seq, a.shape[-1])
    ckv_p, kpe_p, ckv_s, kpe_s = [], [], [], []
    swk_p, swv_p, swk_s, swv_s = [], [], [], []
    for i in range(depth):
        l = i // 2
        if i % 2 == 0:
            cq, ckv, ckvb, kpe2, kpeb, sg = _mla_in(hp, pre, m_w_in, q_gain, kv_gain, cs_p,
                                                    ROW_TILE, i, l)
            qt, kn, vt = _mla_qkv(cq, ckvb, wqnt, wqrt, wqrott, wuk, wuvt, cct_p, sst_p,
                                  QKV_ROW_TILE, l)
            og = _mla_flash(qt, to3(kn), to3(kpeb), vt, to3(sg),
                            FLASH_Q_BLOCK, FLASH_KEY_CHUNK, FLASH_HEADS)
            hp = _out_proj(og.reshape(n_b * seq, MLA_WIDTH), m_w_o, post, hp, ROW_TILE, l, i)
            ckv_p.append(ckv.reshape(n_b, seq, KV_LORA))
            kpe_p.append(kpe2[:, :QK_ROPE].reshape(n_b, seq, QK_ROPE))
            cq, ckv, _, kpe2, _, sg = _mla_in(hs, pre, m_w_in, q_gain, kv_gain, cs_s, n_db, i, l)
            qn, qr = _mla_q(cq, wqn, wqr, wqrot, cc_s, ss_s, l)
            qlat = _qlat(qn, wukt, l)
            kpe_new = kpe2[:, :QK_ROPE]
            olat = _paged_attention(
                page_table, qlat.reshape(n_db, MLA_HEADS, KV_LORA),
                qr.reshape(n_db, MLA_HEADS, QK_ROPE), ckv.reshape(n_db, 1, KV_LORA),
                kpe_new.reshape(n_db, 1, QK_ROPE), cache_ckv, cache_kpe_t, l, PAGED_CHUNK_PAGES)
            og = _olat_proj(olat.reshape(n_db, MLA_HEADS * KV_LORA), wuv3, sg, l)
            hs = _out_proj(og, m_w_o, post, hs, n_db, l, i)
            ckv_s.append(ckv.reshape(n_db, 1, KV_LORA))
            kpe_s.append(kpe_new.reshape(n_db, 1, QK_ROPE))
        else:
            q, k, v, sg = _swa_in(hp, pre, s_w_in, ROW_TILE, i, l)
            og = _swa_prompt(swa_sinks, to3(q), to3(k), to3(v), to3(sg), l)
            hp = _out_proj(og.reshape(n_b * seq, SWA_Q), s_w_o, post, hp, ROW_TILE, l, i)
            keep = min(WINDOW, seq)
            tail = lambda a: to3(a)[:, seq - keep:].reshape(n_b, keep, SWA_KV_HEADS, SWA_HEAD_DIM)
            swk_p.append(tail(k))
            swv_p.append(tail(v))
            q, k, v, sg = _swa_in(hs, pre, s_w_in, n_db, i, l)
            nk, nv, og3 = _swa_sample(
                q.reshape(n_db, SWA_HEADS, SWA_HEAD_DIM), buf_k, buf_v,
                k.reshape(n_db, 1, SWA_KV), v.reshape(n_db, 1, SWA_KV),
                sg.reshape(n_db, SWA_HEADS, SWA_HEAD_DIM),
                sink_cols, slope_col, mask, l, SWA_SAMPLE_SEQS)
            hs = _out_proj(og3.reshape(n_db, SWA_Q), s_w_o, post, hs, n_db, l, i)
            swk_s.append(nk.reshape(n_db, WINDOW, SWA_KV_HEADS, SWA_HEAD_DIM))
            swv_s.append(nv.reshape(n_db, WINDOW, SWA_KV_HEADS, SWA_HEAD_DIM))
    return (hp.reshape(n_b, seq, D_MODEL), hs.reshape(n_db, t_new, D_MODEL),
            jnp.stack(ckv_p), jnp.stack(kpe_p), jnp.stack(ckv_s), jnp.stack(kpe_s),
            jnp.stack(swk_p), jnp.stack(swv_p), jnp.stack(swk_s), jnp.stack(swv_s))
```

```python
import functools
import math

import jax
import jax.numpy as jnp
import numpy as np
from jax import lax
from jax.experimental import pallas as pl
from jax.experimental.pallas import tpu as pltpu

F32 = jnp.float32
BF16 = jnp.bfloat16

D_MODEL = 1024
NORM_EPS = 1e-6
NEG_INF = -1e30
PAGE_SIZE = 128

MLA_HEADS = 16
Q_LORA = 512
KV_LORA = 256
QK_NOPE = 128
QK_ROPE = 64
V_HEAD = 128
ROPE_THETA = 10000.0
MLA_WIDTH = MLA_HEADS * V_HEAD
MLA_SCALE = 1.0 / math.sqrt(QK_NOPE + QK_ROPE)
MLA_QSCALE = MLA_SCALE * math.log2(math.e)
Q_ROWS = QK_NOPE + 128

SWA_HEADS = 16
SWA_KV_HEADS = 4
SWA_GROUP = SWA_HEADS // SWA_KV_HEADS
SWA_HEAD_DIM = 64
WINDOW = 128
SWA_Q = SWA_HEADS * SWA_HEAD_DIM
SWA_KV = SWA_KV_HEADS * SWA_HEAD_DIM
SWA_SCALE = 1.0 / math.sqrt(SWA_HEAD_DIM)

LANES = 128
VMEM_LIMIT = 56 * 1024 * 1024

ROW_TILE = 1024
QKV_ROW_TILE = 512
FLASH_Q_BLOCK = 512
FLASH_KEY_CHUNK = 256
FLASH_HEADS = 4
PAGED_CHUNK_PAGES = 16
SWA_SAMPLE_SEQS = 8

SWA_HEAD_ORDER = tuple(
    (2 * i + par) * SWA_GROUP + g
    for i in range(SWA_KV_HEADS // 2) for g in range(SWA_GROUP) for par in range(2))

NT_DIMS = (((1,), (1,)), ((), ()))


def _params(*sem):
    return pltpu.CompilerParams(dimension_semantics=sem, vmem_limit_bytes=VMEM_LIMIT)


def _rms(x, g):
    return x * lax.rsqrt(jnp.mean(x * x, axis=-1, keepdims=True) + NORM_EPS) * g


def _silu(x):
    return x * (1.0 / (1.0 + jnp.exp(-x)))


def _full(shape):
    return pl.BlockSpec(shape, lambda *_: (0,) * len(shape))


def _of_layer(stacked, layer):
    shape = stacked.shape[1:]
    return pl.BlockSpec((None,) + shape, lambda *_: (layer,) + (0,) * len(shape))


def _mla_in_kernel(h_ref, g_ref, w_ref, qn_ref, kvn_ref, cs_ref,
                   cq_ref, ckv_ref, ckvb_ref, kpe_ref, kpeb_ref, sg_ref):
    a = _rms(h_ref[...], g_ref[...]).astype(BF16)
    z = jnp.dot(a, w_ref[...], preferred_element_type=F32)
    cq_ref[...] = _rms(z[:, :Q_LORA], qn_ref[...]).astype(BF16)
    ckv = _rms(z[:, Q_LORA:Q_LORA + KV_LORA], kvn_ref[...])
    ckv_ref[...] = ckv
    ckvb_ref[...] = ckv.astype(BF16)
    g0 = Q_LORA + KV_LORA
    sg_ref[...] = _silu(z[:, g0:g0 + MLA_WIDTH])
    t = z[:, g0 + MLA_WIDTH:] * cs_ref[...]
    kpe2 = t + pltpu.roll(t, QK_ROPE, axis=1)
    kpe_ref[...] = kpe2
    kpeb_ref[...] = kpe2.astype(BF16)


def _mla_in(h, gains, w, qn, kvn, cs, tm, depth_i, layer):
    rows = h.shape[0]
    n_cs = cs.shape[0] // tm
    row = lambda r: (r, 0)
    outs = (
        jax.ShapeDtypeStruct((rows, Q_LORA), BF16),
        jax.ShapeDtypeStruct((rows, KV_LORA), F32),
        jax.ShapeDtypeStruct((rows, KV_LORA), BF16),
        jax.ShapeDtypeStruct((rows, LANES), F32),
        jax.ShapeDtypeStruct((rows, LANES), BF16),
        jax.ShapeDtypeStruct((rows, MLA_WIDTH), F32),
    )
    return pl.pallas_call(
        _mla_in_kernel,
        grid=(rows // tm,),
        in_specs=[
            pl.BlockSpec((tm, D_MODEL), row),
            _of_layer(gains, depth_i), _of_layer(w, layer),
            _of_layer(qn, layer), _of_layer(kvn, layer),
            pl.BlockSpec((tm, LANES), lambda r: (r % n_cs, 0)),
        ],
        out_specs=tuple(pl.BlockSpec((tm, o.shape[1]), row) for o in outs),
        out_shape=outs,
        compiler_params=_params("parallel"),
        name="mla_in",
    )(h, gains, w, qn, kvn, cs)


def _rope_groups(cq, wr_ref, wrot_ref, cc, ss):
    r = jnp.dot(cq, wr_ref[...], preferred_element_type=F32)
    rr = jnp.dot(cq, wrot_ref[...], preferred_element_type=F32)
    n_grp = r.shape[1] // LANES
    return [r[:, j * LANES:(j + 1) * LANES] * cc + rr[:, j * LANES:(j + 1) * LANES] * ss
            for j in range(n_grp)]


def _mla_qkv_kernel(cq_ref, ckvb_ref, wqnt_ref, wqrt_ref, wqrott_ref, wuk_ref, wuvt_ref,
                    cct_ref, sst_ref, qt_ref, kn_ref, vt_ref):
    cq = cq_ref[...]
    qnt = lax.dot_general(wqnt_ref[...], cq, NT_DIMS, preferred_element_type=F32)
    rt = lax.dot_general(wqrt_ref[...], cq, NT_DIMS, preferred_element_type=F32)
    rrt = lax.dot_general(wqrott_ref[...], cq, NT_DIMS, preferred_element_type=F32)
    cct, sst = cct_ref[...], sst_ref[...]
    low = lax.broadcasted_iota(jnp.int32, cct.shape, 0) < QK_ROPE
    for j in range(MLA_HEADS // 2):
        rows = slice(j * LANES, (j + 1) * LANES)
        grp = (rt[rows, :] * cct + rrt[rows, :] * sst) * MLA_QSCALE
        for par, rope in enumerate((jnp.where(low, grp, 0.0), jnp.where(low, 0.0, grp))):
            h = 2 * j + par
            nope = qnt[h * QK_NOPE:(h + 1) * QK_NOPE, :] * MLA_QSCALE
            qt_ref[h * Q_ROWS:h * Q_ROWS + QK_NOPE, :] = nope.astype(BF16)
            qt_ref[h * Q_ROWS + QK_NOPE:(h + 1) * Q_ROWS, :] = rope.astype(BF16)
    ckv = ckvb_ref[...]
    kn_ref[...] = jnp.dot(ckv, wuk_ref[...], preferred_element_type=F32).astype(BF16)
    vt_ref[...] = lax.dot_general(wuvt_ref[...], ckv, NT_DIMS,
                                  preferred_element_type=F32).astype(BF16)


def _mla_qkv(cq, ckvb, wqnt, wqrt, wqrott, wuk, wuvt, cct, sst, tm, layer):
    rows = cq.shape[0]
    n_cs = cct.shape[1] // tm
    row = lambda r: (r, 0)
    col = lambda r: (0, r)
    tab = pl.BlockSpec((LANES, tm), lambda r: (0, r % n_cs))
    return pl.pallas_call(
        _mla_qkv_kernel,
        grid=(rows // tm,),
        in_specs=[
            pl.BlockSpec((tm, Q_LORA), row),
            pl.BlockSpec((tm, KV_LORA), row),
            _of_layer(wqnt, layer), _of_layer(wqrt, layer), _of_layer(wqrott, layer),
            _of_layer(wuk, layer), _of_layer(wuvt, layer), tab, tab,
        ],
        out_specs=(pl.BlockSpec((MLA_HEADS * Q_ROWS, tm), col),
                   pl.BlockSpec((tm, MLA_WIDTH), row),
                   pl.BlockSpec((MLA_WIDTH, tm), col)),
        out_shape=(jax.ShapeDtypeStruct((MLA_HEADS * Q_ROWS, rows), BF16),
                   jax.ShapeDtypeStruct((rows, MLA_WIDTH), BF16),
                   jax.ShapeDtypeStruct((MLA_WIDTH, rows), BF16)),
        compiler_params=_params("parallel"),
        name="mla_qkv",
    )(cq, ckvb, wqnt, wqrt, wqrott, wuk, wuvt, cct, sst)


def _mla_q_kernel(cq_ref, wqn_ref, wqr_ref, wqrot_ref, cc_ref, ss_ref, qn_ref, qr_ref):
    cq = cq_ref[...]
    qn_ref[...] = jnp.dot(cq, wqn_ref[...], preferred_element_type=F32).astype(BF16)
    groups = _rope_groups(cq, wqr_ref, wqrot_ref, cc_ref[...], ss_ref[...])
    for j, grp in enumerate(groups):
        qr_ref[:, j * LANES:(j + 1) * LANES] = grp.astype(BF16)


def _mla_q(cq, wqn, wqr, wqrot, cc, ss, layer):
    rows = cq.shape[0]
    return pl.pallas_call(
        _mla_q_kernel,
        grid=(1,),
        in_specs=[_full(cq.shape), _of_layer(wqn, layer), _of_layer(wqr, layer),
                  _of_layer(wqrot, layer), _full(cc.shape), _full(ss.shape)],
        out_specs=(_full((rows, MLA_WIDTH)), _full((rows, MLA_HEADS * QK_ROPE))),
        out_shape=(jax.ShapeDtypeStruct((rows, MLA_WIDTH), BF16),
                   jax.ShapeDtypeStruct((rows, MLA_HEADS * QK_ROPE), BF16)),
        compiler_params=_params("arbitrary"),
        name="mla_q_sample",
    )(cq, wqn, wqr, wqrot, cc, ss)


def _mla_flash_kernel(qt_ref, kn_ref, kpe_ref, vt_ref, sg_ref, o_ref, s_scr, mx_scr,
                      acc_scr, *, blk, sub, heads):
    i = pl.program_id(2)
    lanes = [slice(h * LANES, (h + 1) * LANES) for h in range(heads)]
    qt = [qt_ref[h * Q_ROWS:(h + 1) * Q_ROWS, :] for h in range(heads)]

    def scores(h, start, col0, slot, diagonal):
        rows = pl.ds(start, sub)
        k = jnp.concatenate([kn_ref[rows, lanes[h]], kpe_ref[rows, :]], axis=-1)
        s = jnp.dot(k, qt[h][:, col0:], preferred_element_type=F32)
        if diagonal:
            s = causal(s)
        s_scr[h, slot, :, col0:] = s
        mx_scr[h, slot, :, col0:] = jnp.max(s, axis=0, keepdims=True)

    def causal(s):
        ki = lax.broadcasted_iota(jnp.int32, s.shape, 0)
        qi = lax.broadcasted_iota(jnp.int32, s.shape, 1)
        return jnp.where(ki <= qi, s, NEG_INF)

    def fold(h, slot, start, col0, carry, mask_now):
        m, l = (c[:, col0:] for c in carry)
        rows = pl.ds(start, sub)
        s = s_scr[h, slot, :, col0:]
        if mask_now:
            s = causal(s)
            m_new = jnp.maximum(m, jnp.max(s, axis=0, keepdims=True))
        else:
            m_new = jnp.maximum(m, mx_scr[h, slot, :, col0:])
        alpha = jnp.exp2(m - m_new)
        p = jnp.exp2(s - m_new)
        l = alpha * l + jnp.sum(p, axis=0, keepdims=True)
        acc_scr[h, :, col0:] = alpha * acc_scr[h, :, col0:] + jnp.dot(
            vt_ref[lanes[h], rows], p.astype(BF16), preferred_element_type=F32)
        if col0 == 0:
            return m_new, l
        return tuple(jnp.concatenate([c[:, :col0], n], axis=1)
                     for c, n in zip(carry, (m_new, l)))

    n_sub = blk // sub

    def full_block(j, carries):
        base = pl.multiple_of(j * blk, blk)
        carries = list(carries)
        for u in range(n_sub):
            for h in range(heads):
                scores(h, base + (u + 1) * sub, 0, (u + 1) % 2, False)
            for h in range(heads):
                carries[h] = fold(h, u % 2, base + u * sub, 0, carries[h], False)
        return tuple(carries)

    init = (jnp.full((1, blk), NEG_INF, F32), jnp.zeros((1, blk), F32))
    acc_scr[...] = jnp.zeros(acc_scr.shape, F32)
    for h in range(heads):
        scores(h, 0, 0, 0, False)
    carries = list(lax.fori_loop(0, i, full_block, (init,) * heads))
    base = pl.multiple_of(i * blk, blk)
    for u in range(n_sub):
        if u + 1 < n_sub:
            for h in range(heads):
                scores(h, base + (u + 1) * sub, (u + 1) * sub, (u + 1) % 2, True)
        for h in range(heads):
            carries[h] = fold(h, u % 2, base + u * sub, u * sub, carries[h], u == 0)
    for h in range(heads):
        _, l = carries[h]
        o_ref[:, lanes[h]] = ((acc_scr[h] / l).T * sg_ref[:, lanes[h]]).astype(BF16)


def _mla_flash(qt, kn, kpeb, vt, sg, blk, sub, heads):
    b, s, _ = kn.shape
    width = heads * LANES
    n_q = s // blk
    q_spec = pl.BlockSpec((None, blk, width), lambda bi, h, i: (bi, i, h))
    return pl.pallas_call(
        functools.partial(_mla_flash_kernel, blk=blk, sub=sub, heads=heads),
        grid=(b, MLA_HEADS // heads, n_q),
        in_specs=[pl.BlockSpec((heads * Q_ROWS, blk), lambda bi, h, i: (h, bi * n_q + i)),
                  pl.BlockSpec((None, s, width), lambda bi, h, i: (bi, 0, h)),
                  pl.BlockSpec((None, s, LANES), lambda bi, h, i: (bi, 0, 0)),
                  pl.BlockSpec((heads * V_HEAD, s), lambda bi, h, i: (h, bi)),
                  q_spec],
        out_specs=q_spec,
        out_shape=jax.ShapeDtypeStruct((b, s, MLA_WIDTH), BF16),
        scratch_shapes=[pltpu.VMEM((heads, 2, sub, blk), F32),
                        pltpu.VMEM((heads, 2, 1, blk), F32),
                        pltpu.VMEM((heads, V_HEAD, blk), F32)],
        compiler_params=_params("parallel", "parallel", "arbitrary"),
        name="mla_flash",
    )(qt, kn, kpeb, vt, sg)


def _out_kernel(x_ref, w_ref, g_ref, h_ref, o_ref):
    y = jnp.dot(x_ref[...], w_ref[...], preferred_element_type=F32)
    o_ref[...] = h_ref[...] + _rms(y, g_ref[...])


def _out_proj(x, w, gains, h, tm, layer, depth_i):
    rows, width = x.shape
    row = lambda r: (r, 0)
    return pl.pallas_call(
        _out_kernel,
        grid=(rows // tm,),
        in_specs=[pl.BlockSpec((tm, width), row), _of_layer(w, layer), _of_layer(gains, depth_i),
                  pl.BlockSpec((tm, D_MODEL), row)],
        out_specs=pl.BlockSpec((tm, D_MODEL), row),
        out_shape=jax.ShapeDtypeStruct((rows, D_MODEL), F32),
        compiler_params=_params("parallel"),
        name="out_proj",
    )(x, w, gains, h)


def _qlat_kernel(qn_ref, wukt_ref, o_ref):
    o_ref[...] = jnp.dot(qn_ref[...], wukt_ref[...], preferred_element_type=F32).astype(BF16)


def _qlat(qn, wukt, layer):
    rows = qn.shape[0]
    return pl.pallas_call(
        _qlat_kernel,
        grid=(MLA_HEADS,),
        in_specs=[pl.BlockSpec((rows, QK_NOPE), lambda h: (0, h)),
                  pl.BlockSpec((None, None, QK_NOPE, KV_LORA), lambda h: (layer, h, 0, 0))],
        out_specs=pl.BlockSpec((rows, KV_LORA), lambda h: (0, h)),
        out_shape=jax.ShapeDtypeStruct((rows, MLA_HEADS * KV_LORA), BF16),
        compiler_params=_params("parallel"),
        name="mla_qlat",
    )(qn, wukt)


def _paged_kernel(pt_ref, ql_ref, qr_ref, cnew_ref, knew_ref, ckv_hbm, kpe_hbm, o_ref,
                  ckv0, ckv1, kpe0, kpe1, kb, kbt, pb, s_scr, e_scr, sem, *, layer, n_pages,
                  chunk):
    g = pl.program_id(0)
    n_seq = 2 * pl.num_programs(0)
    bufs = ((ckv0, kpe0), (ckv1, kpe1))

    def page_copies(seq, sl, p):
        pid = pt_ref[seq, p]
        ckv_buf, kpe_buf = bufs[sl]
        return (pltpu.make_async_copy(ckv_hbm.at[layer, pid],
                                      ckv_buf.at[pl.ds(p * PAGE_SIZE, PAGE_SIZE), :],
                                      sem.at[0, sl]),
                pltpu.make_async_copy(kpe_hbm.at[layer, pid], kpe_buf.at[p], sem.at[1, sl]))

    def start_page(seq, sl, p):
        for i, copy in enumerate(page_copies(seq, sl, p)):
            copy.start(priority=(p + i) % 2)

    def wait_gather(seq, sl):
        for p in range(n_pages):
            for copy in page_copies(seq, sl, p):
                copy.wait()

    def chunk_keys(c):
        return slice(c * chunk * PAGE_SIZE, (c + 1) * chunk * PAGE_SIZE)

    def score_chunk(sl, c, nxt):
        ckv_buf, kpe_buf = bufs[sl]
        for p in range(c * chunk, (c + 1) * chunk):
            start_page(nxt, 1 - sl, p)
            keys = slice(p * PAGE_SIZE, (p + 1) * PAGE_SIZE)
            page = ckv_buf[keys, :]
            page_b = page.astype(BF16)
            kb[sl, keys, :] = page_b
            kbt[:, keys] = page_b.T
            pb[:, keys] = kpe_buf[p].astype(BF16)
        keys = chunk_keys(c)
        s_scr[:, keys] = (
            jnp.dot(ql_ref[sl], kbt[:, keys], preferred_element_type=F32)
            + jnp.dot(qr_ref[sl], pb[:, keys], preferred_element_type=F32)) * MLA_SCALE

    def softmax(sl):
        s = s_scr[...]
        cn = cnew_ref[sl].astype(BF16).astype(F32)
        kn = knew_ref[sl].astype(BF16).astype(F32)
        s_new = (jnp.sum(ql_ref[sl].astype(F32) * cn, axis=-1, keepdims=True)
                 + jnp.sum(qr_ref[sl].astype(F32) * kn, axis=-1, keepdims=True)) * MLA_SCALE
        m = jnp.maximum(jnp.max(s, axis=-1, keepdims=True), s_new)
        e = jnp.exp(s - m)
        e_new = jnp.exp(s_new - m)
        e_scr[sl] = e.astype(BF16)
        return e_new.astype(BF16).astype(F32) * cn, jnp.sum(e, axis=-1, keepdims=True) + e_new

    def values(sl, c):
        keys = chunk_keys(c)
        return jnp.dot(e_scr[sl, :, keys], kb[sl, keys, :], preferred_element_type=F32)

    @pl.when(g == 0)
    def _():
        for p in range(n_pages):
            start_page(0, 0, p)

    n_chunks = n_pages // chunk
    first = 2 * g
    after = jnp.minimum(first + 2, n_seq - 1)
    wait_gather(first, 0)
    for c in range(n_chunks):
        score_chunk(0, c, first + 1)
    acc, den = softmax(0)
    wait_gather(first + 1, 1)
    for c in range(n_chunks):
        score_chunk(1, c, after)
        acc = acc + values(0, c)
    o_ref[0] = acc / den
    acc, den = softmax(1)
    for c in range(n_chunks):
        acc = acc + values(1, c)
    o_ref[1] = acc / den

    @pl.when(g == pl.num_programs(0) - 1)
    def _():
        wait_gather(n_seq - 1, 0)


def _paged_attention(page_table, qlat3, qr3, ckv_new, kpe_new, cache_ckv, cache_kpe_t, layer,
                     chunk):
    nb, n_pages = page_table.shape
    n_keys = n_pages * PAGE_SIZE
    pair = lambda shape: pl.BlockSpec((2,) + shape, lambda g, pt: (g, 0, 0))
    hbm = pl.BlockSpec(memory_space=pl.ANY)
    grid_spec = pltpu.PrefetchScalarGridSpec(
        num_scalar_prefetch=1,
        grid=(nb // 2,),
        in_specs=[pair((MLA_HEADS, KV_LORA)), pair((MLA_HEADS, QK_ROPE)),
                  pair((1, KV_LORA)), pair((1, QK_ROPE)), hbm, hbm],
        out_specs=pair((MLA_HEADS, KV_LORA)),
        scratch_shapes=[
            pltpu.VMEM((n_keys, KV_LORA), F32),
            pltpu.VMEM((n_keys, KV_LORA), F32),
            pltpu.VMEM((n_pages, QK_ROPE, PAGE_SIZE), F32),
            pltpu.VMEM((n_pages, QK_ROPE, PAGE_SIZE), F32),
            pltpu.VMEM((2, n_keys, KV_LORA), BF16),
            pltpu.VMEM((KV_LORA, n_keys), BF16),
            pltpu.VMEM((QK_ROPE, n_keys), BF16),
            pltpu.VMEM((MLA_HEADS, n_keys), F32),
            pltpu.VMEM((2, MLA_HEADS, n_keys), BF16),
            pltpu.SemaphoreType.DMA((2, 2)),
        ],
    )
    return pl.pallas_call(
        functools.partial(_paged_kernel, layer=layer, n_pages=n_pages, chunk=chunk),
        grid_spec=grid_spec,
        out_shape=jax.ShapeDtypeStruct((nb, MLA_HEADS, KV_LORA), F32),
        compiler_params=_params("arbitrary"),
        name="mla_paged",
    )(page_table, qlat3, qr3, ckv_new, kpe_new, cache_ckv, cache_kpe_t)


def _olat_kernel(ol_ref, wuv_ref, sg_ref, o_ref):
    o = jnp.dot(ol_ref[...].astype(BF16), wuv_ref[...], preferred_element_type=F32)
    o_ref[...] = (o * sg_ref[...]).astype(BF16)


def _olat_proj(olat, wuv3, sg, layer):
    rows = olat.shape[0]
    return pl.pallas_call(
        _olat_kernel,
        grid=(MLA_HEADS,),
        in_specs=[pl.BlockSpec((rows, KV_LORA), lambda h: (0, h)),
                  pl.BlockSpec((None, None, KV_LORA, V_HEAD), lambda h: (layer, h, 0, 0)),
                  pl.BlockSpec((rows, V_HEAD), lambda h: (0, h))],
        out_specs=pl.BlockSpec((rows, V_HEAD), lambda h: (0, h)),
        out_shape=jax.ShapeDtypeStruct((rows, MLA_WIDTH), BF16),
        compiler_params=_params("parallel"),
        name="mla_olat",
    )(olat, wuv3, sg)


def _swa_in_kernel(h_ref, g_ref, w_ref, q_ref, k_ref, v_ref, sg_ref):
    a = _rms(h_ref[...], g_ref[...]).astype(BF16)
    z = jnp.dot(a, w_ref[...], preferred_element_type=F32)
    q_ref[...] = (z[:, :SWA_Q] * SWA_SCALE).astype(BF16)
    k_ref[...] = z[:, SWA_Q:SWA_Q + SWA_KV]
    v_ref[...] = z[:, SWA_Q + SWA_KV:SWA_Q + 2 * SWA_KV]
    sg_ref[...] = _silu(z[:, SWA_Q + 2 * SWA_KV:])


def _swa_in(h, gains, w, tm, depth_i, layer):
    rows = h.shape[0]
    row = lambda r: (r, 0)
    outs = (jax.ShapeDtypeStruct((rows, SWA_Q), BF16),
            jax.ShapeDtypeStruct((rows, SWA_KV), F32),
            jax.ShapeDtypeStruct((rows, SWA_KV), F32),
            jax.ShapeDtypeStruct((rows, SWA_Q), F32))
    return pl.pallas_call(
        _swa_in_kernel,
        grid=(rows // tm,),
        in_specs=[pl.BlockSpec((tm, D_MODEL), row), _of_layer(gains, depth_i),
                  _of_layer(w, layer)],
        out_specs=tuple(pl.BlockSpec((tm, o.shape[1]), row) for o in outs),
        out_shape=outs,
        compiler_params=_params("parallel"),
        name="swa_in",
    )(h, gains, w)


def _sink_softmax(s, sink):
    m = jnp.maximum(jnp.max(s, axis=-1, keepdims=True), sink)
    e = jnp.exp(s - m)
    return e / (jnp.sum(e, axis=-1, keepdims=True) + jnp.exp(sink - m))


def _alibi_slope(head):
    return 2.0 ** (-8.0 * (head + 1) / SWA_HEADS)


def _swa_bias():
    dist = (WINDOW + np.arange(WINDOW))[None, :] - np.arange(2 * WINDOW)[:, None]
    in_window = (dist >= 0) & (dist < WINDOW)
    slopes = np.asarray([_alibi_slope(h) for h in range(SWA_HEADS)], np.float32)
    bias = -slopes[:, None, None] * dist[None].astype(np.float32)
    return np.where(in_window[None], bias, np.float32(NEG_INF)).astype(np.float32)


def _swa_prompt_kernel(sink_ref, q_ref, kp_ref, kc_ref, vp_ref, vc_ref, sg_ref, bias_ref, o_ref,
                       *, layer):
    n = pl.program_id(1)
    kk = jnp.concatenate([kp_ref[...], kc_ref[...]], axis=0).astype(BF16)
    vvt = jnp.concatenate([vp_ref[...], vc_ref[...]], axis=0).T.astype(BF16)
    k_low = lax.broadcasted_iota(jnp.int32, (2 * WINDOW, LANES), 1) < SWA_HEAD_DIM
    v_low = lax.broadcasted_iota(jnp.int32, (LANES, 2 * WINDOW), 0) < SWA_HEAD_DIM
    zero = jnp.zeros((), BF16)
    prev_pen = jnp.where(n > 0, 0.0, NEG_INF).astype(F32)
    span = lambda g: slice(g * WINDOW, (g + 1) * WINDOW)
    pairs = range(SWA_KV_HEADS // 2)
    kv_span = [slice(pair * LANES, (pair + 1) * LANES) for pair in pairs]
    grp_lanes = [[slice((pair * SWA_GROUP + g) * LANES, (pair * SWA_GROUP + g + 1) * LANES)
                  for g in range(SWA_GROUP)] for pair in pairs]
    st_all = {}
    for pair in pairs:
        q_st = jnp.concatenate([q_ref[:, lanes] for lanes in grp_lanes[pair]], axis=0)
        for par in range(2):
            keep_k = k_low if par == 0 else ~k_low
            st_all[pair, par] = lax.dot_general(
                jnp.where(keep_k, kk[:, kv_span[pair]], zero), q_st, NT_DIMS,
                preferred_element_type=F32)
    for pair in pairs:
        vt_g = vvt[kv_span[pair], :]
        ot_pair = jnp.zeros((LANES, SWA_GROUP * WINDOW), F32)
        for par in range(2):
            keep_v = v_low if par == 0 else ~v_low
            es, inv_dens = [], []
            for g in range(SWA_GROUP):
                head = SWA_HEAD_ORDER[2 * (pair * SWA_GROUP + g) + par]
                s = st_all[pair, par][:, span(g)] + bias_ref[head]
                s_prev = s[:WINDOW, :] + prev_pen
                s_cur = s[WINDOW:, :]
                sink = sink_ref[layer, head]
                m = jnp.maximum(jnp.maximum(jnp.max(s_prev, axis=0, keepdims=True),
                                            jnp.max(s_cur, axis=0, keepdims=True)), sink)
                e_prev = jnp.exp(s_prev - m)
                e_cur = jnp.exp(s_cur - m)
                den = (jnp.sum(e_prev, axis=0, keepdims=True)
                       + jnp.sum(e_cur, axis=0, keepdims=True) + jnp.exp(sink - m))
                es.append(jnp.concatenate([e_prev, e_cur], axis=0).astype(BF16))
                inv_dens.append(1.0 / den)
            ot_par = jnp.dot(jnp.where(keep_v, vt_g, zero), jnp.concatenate(es, axis=1),
                             preferred_element_type=F32)
            ot_pair = ot_pair + ot_par * jnp.concatenate(inv_dens, axis=1)
        o_pair = ot_pair.T
        for g, lanes in enumerate(grp_lanes[pair]):
            o_ref[:, lanes] = (o_pair[span(g), :] * sg_ref[:, lanes]).astype(BF16)


def _swa_prompt(sinks, q, k, v, sg, layer):
    b, s, _ = q.shape
    cur = lambda bi, n: (bi, n, 0)
    prev = lambda bi, n: (bi, jnp.maximum(n - 1, 0), 0)
    wide = lambda idx: pl.BlockSpec((None, WINDOW, SWA_Q), idx)
    narrow = lambda idx: pl.BlockSpec((None, WINDOW, SWA_KV), idx)
    bias = _swa_bias()
    return pl.pallas_call(
        functools.partial(_swa_prompt_kernel, layer=layer),
        grid=(b, s // WINDOW),
        in_specs=[pl.BlockSpec(memory_space=pltpu.SMEM),
                  wide(cur), narrow(prev), narrow(cur), narrow(prev), narrow(cur), wide(cur),
                  _full(bias.shape)],
        out_specs=wide(cur),
        out_shape=jax.ShapeDtypeStruct((b, s, SWA_Q), BF16),
        compiler_params=_params("parallel", "parallel"),
        name="swa_prompt",
    )(sinks, q, k, k, v, v, sg, jnp.asarray(bias))


def _swa_sample_kernel(q_ref, bk_ref, bv_ref, kn_ref, vn_ref, sg_ref, sink_ref, slope_ref,
                       mask_ref, nk_ref, nv_ref, o_ref):
    n_seq = q_ref.shape[0]
    mask = mask_ref[...] > 0.5
    scores = []
    for j in range(n_seq):
        nk_ref[j, 0:WINDOW - 1, :] = bk_ref[j, 1:WINDOW, :]
        nk_ref[j, WINDOW - 1:WINDOW, :] = kn_ref[j]
        nv_ref[j, 0:WINDOW - 1, :] = bv_ref[j, 1:WINDOW, :]
        nv_ref[j, WINDOW - 1:WINDOW, :] = vn_ref[j]
        q_exp = jnp.where(mask, jnp.concatenate([q_ref[j]] * SWA_KV_HEADS, axis=-1),
                          jnp.zeros((), BF16))
        scores.append(lax.dot_general(q_exp, nk_ref[j].astype(BF16), NT_DIMS,
                                      preferred_element_type=F32))
    dist = (WINDOW - 1 - lax.broadcasted_iota(jnp.int32, scores[0].shape, 1)).astype(F32)
    bias = slope_ref[...] * dist
    probs = [_sink_softmax(s - bias, sink_ref[...]).astype(BF16) for s in scores]
    for j in range(n_seq):
        o_all = jnp.where(mask, jnp.dot(probs[j], nv_ref[j].astype(BF16),
                                        preferred_element_type=F32), 0.0)
        o = o_all[:, 0:SWA_HEAD_DIM]
        for kh in range(1, SWA_KV_HEADS):
            o = o + o_all[:, kh * SWA_HEAD_DIM:(kh + 1) * SWA_HEAD_DIM]
        o_ref[j] = (o * sg_ref[j]).astype(BF16)


def _swa_sample(q3, buf_k, buf_v, k_new, v_new, sg3, sink_col, slope_col, mask, layer, per_step):
    nb = q3.shape[0]
    per_b = lambda shape: pl.BlockSpec((per_step,) + shape, lambda b: (b, 0, 0))
    buf = pl.BlockSpec((None, per_step, WINDOW, SWA_KV), lambda b: (layer, b, 0, 0))
    return pl.pallas_call(
        _swa_sample_kernel,
        grid=(nb // per_step,),
        in_specs=[per_b((SWA_HEADS, SWA_HEAD_DIM)), buf, buf,
                  per_b((1, SWA_KV)), per_b((1, SWA_KV)), per_b((SWA_HEADS, SWA_HEAD_DIM)),
                  _of_layer(sink_col, layer), _full((SWA_HEADS, 1)),
                  _full((SWA_HEADS, SWA_KV))],
        out_specs=(per_b((WINDOW, SWA_KV)), per_b((WINDOW, SWA_KV)),
                   per_b((SWA_HEADS, SWA_HEAD_DIM))),
        out_shape=(jax.ShapeDtypeStruct((nb, WINDOW, SWA_KV), F32),
                   jax.ShapeDtypeStruct((nb, WINDOW, SWA_KV), F32),
                   jax.ShapeDtypeStruct((nb, SWA_HEADS, SWA_HEAD_DIM), BF16)),
        compiler_params=_params("parallel"),
        name="swa_sample",
    )(q3, buf_k, buf_v, k_new, v_new, sg3, sink_col, slope_col, mask)


def _rot_cols(w):
    half = QK_ROPE // 2
    return jnp.concatenate([-w[..., half:], w[..., :half]], axis=-1)


def _rope_tables(pos):
    inv = ROPE_THETA ** (-jnp.arange(0, QK_ROPE, 2, dtype=F32) / QK_ROPE)
    ang = pos.astype(F32)[:, None] * inv[None, :]
    ang = jnp.concatenate([ang, ang], axis=-1)
    cos, sin = jnp.cos(ang), jnp.sin(ang)
    return (jnp.concatenate([cos, sin], axis=-1), jnp.concatenate([cos, cos], axis=-1),
            jnp.concatenate([sin, sin], axis=-1))


def _mla_weights(w_in, w_qb, w_uk, w_uv, w_o):
    n_l = w_in.shape[0]
    g0 = Q_LORA + KV_LORA
    w_kpe = w_in[..., g0:g0 + QK_ROPE]
    w_in_r = jnp.concatenate(
        [w_in[..., :g0], w_in[..., g0 + QK_ROPE:], w_kpe, _rot_cols(w_kpe)], axis=-1).astype(BF16)
    wqn = w_qb[..., :QK_NOPE].reshape(n_l, Q_LORA, MLA_HEADS * QK_NOPE).astype(BF16)
    wqr3 = w_qb[..., QK_NOPE:]
    wqr = wqr3.reshape(n_l, Q_LORA, MLA_HEADS * QK_ROPE).astype(BF16)
    wqrot = _rot_cols(wqr3).reshape(n_l, Q_LORA, MLA_HEADS * QK_ROPE).astype(BF16)
    wuk = w_uk.reshape(n_l, KV_LORA, MLA_HEADS * QK_NOPE).astype(BF16)
    wuvt = jnp.swapaxes(w_uv.reshape(n_l, KV_LORA, MLA_HEADS * V_HEAD), 1, 2).astype(BF16)
    wukt = jnp.transpose(w_uk, (0, 2, 3, 1)).astype(BF16)
    wuv3 = jnp.transpose(w_uv, (0, 2, 1, 3)).astype(BF16)
    return w_in_r, wqn, wqr, wqrot, wuk, wuvt, wukt, wuv3, w_o.astype(BF16)


def _swa_weights(w_in, w_o):
    cols = np.concatenate([np.arange(h * SWA_HEAD_DIM, (h + 1) * SWA_HEAD_DIM)
                           for h in SWA_HEAD_ORDER])
    g0 = SWA_Q + 2 * SWA_KV
    w_in_r = jnp.concatenate(
        [w_in[..., :SWA_Q][..., cols], w_in[..., SWA_Q:g0], w_in[..., g0:][..., cols]],
        axis=-1).astype(BF16)
    return w_in_r, w_o[:, cols, :].astype(BF16)


def kernel(x_prompt, x_sample, cache_ckv, cache_kpe, state_swa_k, state_swa_v, page_table,
           pre_norm, post_norm, mla_w_in, mla_q_norm, mla_w_qb, mla_kv_norm, mla_w_uk,
           mla_w_uv, mla_w_o, swa_w_in, swa_sinks, swa_w_o):
    n_b, seq, _ = x_prompt.shape
    n_db, t_new, _ = x_sample.shape
    n_pages = page_table.shape[1]
    assert t_new == 1 and state_swa_k.shape[2] == WINDOW
    assert seq % ROW_TILE == 0 and seq % FLASH_Q_BLOCK == 0
    assert n_db % 2 == 0 and n_db % SWA_SAMPLE_SEQS == 0 and n_pages % PAGED_CHUNK_PAGES == 0
    depth = pre_norm.shape[0]
    past_len = n_pages * PAGE_SIZE

    cs_p, cc_p, ss_p = _rope_tables(jnp.arange(seq, dtype=jnp.int32))
    cs_s, cc_s, ss_s = (jnp.broadcast_to(t, (n_db, LANES))
                        for t in _rope_tables(jnp.full((1,), past_len, jnp.int32)))

    order = np.asarray(SWA_HEAD_ORDER)
    slope_col = jnp.asarray([[_alibi_slope(h)] for h in SWA_HEAD_ORDER], F32)
    lane_kv = np.arange(SWA_KV)[None, :] // SWA_HEAD_DIM
    mask = jnp.asarray(lane_kv == (order // SWA_GROUP)[:, None], F32)
    sink_cols = swa_sinks[:, order][:, :, None]
    cache_kpe_t = jnp.swapaxes(cache_kpe, 2, 3)
    buf_k = state_swa_k.reshape(state_swa_k.shape[:3] + (SWA_KV,))
    buf_v = state_swa_v.reshape(state_swa_v.shape[:3] + (SWA_KV,))

    pre = pre_norm[:, None, :]
    post = post_norm[:, None, :]
    q_gain = mla_q_norm[:, None, :]
    kv_gain = mla_kv_norm[:, None, :]
    m_w_in, wqn, wqr, wqrot, wuk, wuvt, wukt, wuv3, m_w_o = _mla_weights(
        mla_w_in, mla_w_qb, mla_w_uk, mla_w_uv, mla_w_o)
    s_w_in, s_w_o = _swa_weights(swa_w_in, swa_w_o)
    wqnt, wqrt, wqrott = (jnp.swapaxes(w, 1, 2) for w in (wqn, wqr, wqrot))
    cct_p, sst_p = cc_p.T, ss_p.T

    hp = x_prompt.reshape(n_b * seq, D_MODEL)
    hs = x_sample.reshape(n_db, D_MODEL)
    to3 = lambda a: a.reshape(n_b, seq, a.shape[-1])
    ckv_p, kpe_p, ckv_s, kpe_s = [], [], [], []
    swk_p, swv_p, swk_s, swv_s = [], [], [], []
    for i in range(depth):
        l = i // 2
        if i % 2 == 0:
            cq, ckv, ckvb, kpe2, kpeb, sg = _mla_in(hp, pre, m_w_in, q_gain, kv_gain, cs_p,
                                                    ROW_TILE, i, l)
            qt, kn, vt = _mla_qkv(cq, ckvb, wqnt, wqrt, wqrott, wuk, wuvt, cct_p, sst_p,
                                  QKV_ROW_TILE, l)
            og = _mla_flash(qt, to3(kn), to3(kpeb), vt, to3(sg),
                            FLASH_Q_BLOCK, FLASH_KEY_CHUNK, FLASH_HEADS)
            hp = _out_proj(og.reshape(n_b * seq, MLA_WIDTH), m_w_o, post, hp, ROW_TILE, l, i)
            ckv_p.append(ckv.reshape(n_b, seq, KV_LORA))
            kpe_p.append(kpe2[:, :QK_ROPE].reshape(n_b, seq, QK_ROPE))
            cq, ckv, _, kpe2, _, sg = _mla_in(hs, pre, m_w_in, q_gain, kv_gain, cs_s, n_db, i, l)
            qn, qr = _mla_q(cq, wqn, wqr, wqrot, cc_s, ss_s, l)
            qlat = _qlat(qn, wukt, l)
            kpe_new = kpe2[:, :QK_ROPE]
            olat = _paged_attention(
                page_table, qlat.reshape(n_db, MLA_HEADS, KV_LORA),
                qr.reshape(n_db, MLA_HEADS, QK_ROPE), ckv.reshape(n_db, 1, KV_LORA),
                kpe_new.reshape(n_db, 1, QK_ROPE), cache_ckv, cache_kpe_t, l, PAGED_CHUNK_PAGES)
            og = _olat_proj(olat.reshape(n_db, MLA_HEADS * KV_LORA), wuv3, sg, l)
            hs = _out_proj(og, m_w_o, post, hs, n_db, l, i)
            ckv_s.append(ckv.reshape(n_db, 1, KV_LORA))
            kpe_s.append(kpe_new.reshape(n_db, 1, QK_ROPE))
        else:
            q, k, v, sg = _swa_in(hp, pre, s_w_in, ROW_TILE, i, l)
            og = _swa_prompt(swa_sinks, to3(q), to3(k), to3(v), to3(sg), l)
            hp = _out_proj(og.reshape(n_b * seq, SWA_Q), s_w_o, post, hp, ROW_TILE, l, i)
            keep = min(WINDOW, seq)
            tail = lambda a: to3(a)[:, seq - keep:].reshape(n_b, keep, SWA_KV_HEADS, SWA_HEAD_DIM)
            swk_p.append(tail(k))
            swv_p.append(tail(v))
            q, k, v, sg = _swa_in(hs, pre, s_w_in, n_db, i, l)
            nk, nv, og3 = _swa_sample(
                q.reshape(n_db, SWA_HEADS, SWA_HEAD_DIM), buf_k, buf_v,
                k.reshape(n_db, 1, SWA_KV), v.reshape(n_db, 1, SWA_KV),
                sg.reshape(n_db, SWA_HEADS, SWA_HEAD_DIM),
                sink_cols, slope_col, mask, l, SWA_SAMPLE_SEQS)
            hs = _out_proj(og3.reshape(n_db, SWA_Q), s_w_o, post, hs, n_db, l, i)
            swk_s.append(nk.reshape(n_db, WINDOW, SWA_KV_HEADS, SWA_HEAD_DIM))
            swv_s.append(nv.reshape(n_db, WINDOW, SWA_KV_HEADS, SWA_HEAD_DIM))
    return (hp.reshape(n_b, seq, D_MODEL), hs.reshape(n_db, t_new, D_MODEL),
            jnp.stack(ckv_p), jnp.stack(kpe_p), jnp.stack(ckv_s), jnp.stack(kpe_s),
            jnp.stack(swk_p), jnp.stack(swv_p), jnp.stack(swk_s), jnp.stack(swv_s))
```
